```python
import jax
import jax.numpy as jnp
from jax import lax
import numpy as np

D_MODEL = 1024
BATCH = 8
SEQ = 4096
DEPTH = 4

D_FF = 4 * D_MODEL
N_EVEN = (DEPTH + 1) // 2
N_ODD = DEPTH // 2
NORM_EPS = 1e-6
NEG_INF = -1e30

RET_HEADS = 4
RET_DK = D_MODEL // 8
RET_DV = D_MODEL // 8
RET_CHUNK = 128
ROPE_BASE = 10000.0

SWA_HEADS = 8
SWA_KV_HEADS = 2
SWA_HD = D_MODEL // 16
WINDOW = 128

GLA_HEADS = 4
GLA_DK = D_MODEL // (2 * GLA_HEADS)
GLA_DV = D_MODEL // GLA_HEADS
GLA_RANK = 16
GLA_TAU = 16.0
GLA_CHUNK = 64

EV_SPLITS = (RET_HEADS * RET_DK, RET_HEADS * RET_DK, RET_HEADS * RET_DV, RET_HEADS * RET_DV,
             SWA_HEADS * SWA_HD, SWA_KV_HEADS * SWA_HD, SWA_KV_HEADS * SWA_HD)
EV_IN = sum(EV_SPLITS)
EV_MIX = RET_HEADS * RET_DV + SWA_HEADS * SWA_HD
OD_SPLITS = (GLA_HEADS * GLA_DK, GLA_HEADS * GLA_DK, GLA_HEADS * GLA_DV, GLA_HEADS * GLA_DV, GLA_RANK)
OD_IN = sum(OD_SPLITS)
OD_MIX = GLA_HEADS * GLA_DV

kernel_name = 'hybrid_retention_swa_gla_block'


def split_cols(z, sizes):
    idx = np.cumsum(sizes)[:-1].tolist()
    return jnp.split(z, idx, axis=-1)


def rms_norm(x, g):
    xf = x.astype(jnp.float32)
    y = xf * lax.rsqrt(jnp.mean(xf * xf, axis=-1, keepdims=True) + NORM_EPS)
    return (y * g.astype(jnp.float32)).astype(x.dtype)


def head_rms_norm(o, g):
    H, d = o.shape[-2], o.shape[-1]
    y = o * lax.rsqrt(jnp.mean(o * o, axis=-1, keepdims=True) + NORM_EPS)
    return y * g.astype(jnp.float32).reshape(H, d)


def rotary(x, pos):
    half = x.shape[-1] // 2
    inv = ROPE_BASE ** (-jnp.arange(half, dtype=jnp.float32) / half)
    ang = pos.astype(jnp.float32)[:, None] * inv[None, :]
    cos = jnp.cos(ang)[None, :, None, :]
    sin = jnp.sin(ang)[None, :, None, :]
    xf = x.astype(jnp.float32)
    x1, x2 = xf[..., :half], xf[..., half:]
    return jnp.concatenate([x1 * cos - x2 * sin, x1 * sin + x2 * cos], axis=-1).astype(x.dtype)


def retention(q, k, v):
    B, T, H, dk = q.shape
    dv = v.shape[-1]
    C = RET_CHUNK
    N = T // C
    log_gamma = jnp.log1p(-(2.0 ** (-5.0 - jnp.arange(H, dtype=jnp.float32))))
    qf = q.astype(jnp.float32).reshape(B, N, C, H, dk)
    kf = k.astype(jnp.float32).reshape(B, N, C, H, dk) * (dk ** -0.5)
    vf = v.astype(jnp.float32).reshape(B, N, C, H, dv)
    idx = jnp.arange(C, dtype=jnp.float32)
    rel = idx[:, None] - idx[None, :]
    causal = rel >= 0
    decay = jnp.where(causal[None], jnp.exp(log_gamma[:, None, None] * jnp.where(causal, rel, 0.0)[None]), 0.0)
    scores = jnp.einsum('bnihd,bnjhd->bnhij', qf, kf) * decay[None, None]
    o_intra = jnp.einsum('bnhij,bnjhe->bnihe', scores, vf)
    k_dec = jnp.exp(log_gamma[None, :] * (C - 1 - idx)[:, None])
    kv = jnp.einsum('bnjhd,jh,bnjhe->nbhde', kf, k_dec, vf)
    chunk_decay = jnp.exp(log_gamma * C)[None, :, None, None]

    def step(s, kv_n):
        return chunk_decay * s + kv_n, s

    _, s_prev = lax.scan(step, jnp.zeros((B, H, dk, dv), jnp.float32), kv)
    q_dec = jnp.exp(log_gamma[None, :] * (idx + 1.0)[:, None])
    o_cross = jnp.einsum('bnihd,ih,nbhde->bnihe', qf, q_dec, s_prev)
    return (o_intra + o_cross).reshape(B, T, H, dv)


def swa_with_sinks(q, k, v, sinks):
    B, T, Hq, d = q.shape
    Hkv = k.shape[2]
    G = Hq // Hkv
    C = WINDOW
    N = T // C
    qb = q.astype(jnp.float32).reshape(B, N, C, Hkv, G, d) * (d ** -0.5)
    kb = k.astype(jnp.float32).reshape(B, N, C, Hkv, d)
    vb = v.astype(jnp.float32).reshape(B, N, C, Hkv, d)
    kk = jnp.concatenate([jnp.concatenate([jnp.zeros_like(kb[:, :1]), kb[:, :-1]], axis=1), kb], axis=2)
    vv = jnp.concatenate([jnp.concatenate([jnp.zeros_like(vb[:, :1]), vb[:, :-1]], axis=1), vb], axis=2)
    s = jnp.einsum('bnqhgd,bnkhd->bnhgqk', qb, kk)
    qpos = jnp.arange(C)[:, None] + C
    kpos = jnp.arange(2 * C)[None, :]
    diff = qpos - kpos
    allowed = (diff >= 0) & (diff < WINDOW)
    mask = allowed[None] & ((jnp.arange(N)[:, None, None] > 0) | (kpos[None] >= C))
    s = jnp.where(mask[None, :, None, None], s, NEG_INF)
    sink = sinks.astype(jnp.float32).reshape(Hkv, G)[None, None, :, :, None, None]
    m = jnp.maximum(jnp.max(s, axis=-1, keepdims=True), sink)
    p = jnp.exp(s - m)
    denom = jnp.sum(p, axis=-1, keepdims=True) + jnp.exp(sink - m)
    o = jnp.einsum('bnhgqk,bnkhd->bnqhgd', p / denom, vv)
    return o.reshape(B, T, Hq * d)


def gla(q, k, v, log_g):
    B, T, H, dk = q.shape
    dv = v.shape[-1]
    C = GLA_CHUNK
    N = T // C
    qf = q.astype(jnp.float32).reshape(B, N, C, H, dk) * (dk ** -0.5)
    kf = k.astype(jnp.float32).reshape(B, N, C, H, dk)
    vf = v.astype(jnp.float32).reshape(B, N, C, H, dv)
    b = jnp.cumsum(log_g.astype(jnp.float32).reshape(B, N, C, H, dk), axis=2)
    b_last = b[:, :, -1:]
    q_in = qf * jnp.exp(b)
    k_in = kf * jnp.exp(-b)
    causal = jnp.tril(jnp.ones((C, C), dtype=bool))
    a = jnp.where(causal, jnp.einsum('bnihd,bnjhd->bnhij', q_in, k_in), 0.0)
    o_intra = jnp.einsum('bnhij,bnjhe->bnihe', a, vf)
    kv = jnp.einsum('bnjhd,bnjhe->nbhde', kf * jnp.exp(b_last - b), vf)
    chunk_decay = jnp.moveaxis(jnp.exp(b_last[:, :, 0]), 1, 0)

    def step(s, inp):
        kv_n, dec_n = inp
        return dec_n[..., None] * s + kv_n, s

    _, s_prev = lax.scan(step, jnp.zeros((B, H, dk, dv), jnp.float32), (kv, chunk_decay))
    o_cross = jnp.einsum('bnihd,nbhde->bnihe', q_in, s_prev)
    return (o_intra + o_cross).reshape(B, T, H, dv)


def retention_swa_mixer(u, w_in, ret_norm_g, sinks, w_out, pos):
    B, T, _ = u.shape
    z = u @ w_in
    rq, rk, rv, rg, aq, ak, av = split_cols(z, EV_SPLITS)
    rq = rotary(rq.reshape(B, T, RET_HEADS, RET_DK), pos)
    rk = rotary(rk.reshape(B, T, RET_HEADS, RET_DK), pos)
    o_ret = retention(rq, rk, rv.reshape(B, T, RET_HEADS, RET_DV))
    o_ret = head_rms_norm(o_ret, ret_norm_g).reshape(B, T, RET_HEADS * RET_DV).astype(u.dtype) * jax.nn.silu(rg)
    o_att = swa_with_sinks(aq.reshape(B, T, SWA_HEADS, SWA_HD),
                           ak.reshape(B, T, SWA_KV_HEADS, SWA_HD),
                           av.reshape(B, T, SWA_KV_HEADS, SWA_HD), sinks).astype(u.dtype)
    return jnp.concatenate([o_ret, o_att], axis=-1) @ w_out


def gla_mixer(u, w_in, w_gate_up, gate_bias, norm_g, w_out):
    B, T, _ = u.shape
    z = u @ w_in
    q, k, v, r, a_low = split_cols(z, OD_SPLITS)
    log_g = jax.nn.log_sigmoid((a_low @ w_gate_up + gate_bias).astype(jnp.float32)) / GLA_TAU
    o = gla(q.reshape(B, T, GLA_HEADS, GLA_DK), k.reshape(B, T, GLA_HEADS, GLA_DK),
            v.reshape(B, T, GLA_HEADS, GLA_DV), log_g.reshape(B, T, GLA_HEADS, GLA_DK))
    o = head_rms_norm(o, norm_g).reshape(B, T, OD_MIX).astype(u.dtype) * jax.nn.silu(r)
    return o @ w_out


def sq_relu_mlp(u, w_up, w_down):
    return jnp.square(jax.nn.relu(u @ w_up)) @ w_down


def setup_inputs(seed: int = 0) -> dict:
    key = jax.random.key(seed)
    ks = jax.random.split(key, 14)
    f32 = jnp.float32

    def nrm(k, shape, fan_in):
        return jax.random.normal(k, shape, f32) * (fan_in ** -0.5)

    return {
        'x': jax.random.normal(ks[0], (BATCH, SEQ, D_MODEL), f32),
        'norm_g': 1.0 + 0.02 * jax.random.normal(ks[1], (DEPTH, 4, D_MODEL), f32),
        'w_up': nrm(ks[2], (DEPTH, D_MODEL, D_FF), D_MODEL),
        'w_down': nrm(ks[3], (DEPTH, D_FF, D_MODEL), D_FF),
        'ev_w_in': nrm(ks[4], (N_EVEN, D_MODEL, EV_IN), D_MODEL),
        'ev_ret_norm_g': 1.0 + 0.02 * jax.random.normal(ks[5], (N_EVEN, RET_HEADS * RET_DV), f32),
        'ev_sinks': 0.5 * jax.random.normal(ks[6], (N_EVEN, SWA_HEADS), f32),
        'ev_w_out': nrm(ks[7], (N_EVEN, EV_MIX, D_MODEL), EV_MIX),
        'od_w_in': nrm(ks[8], (N_ODD, D_MODEL, OD_IN), D_MODEL),
        'od_w_gate_up': nrm(ks[9], (N_ODD, GLA_RANK, GLA_HEADS * GLA_DK), GLA_RANK),
        'od_gate_bias': 0.1 * jax.random.normal(ks[10], (N_ODD, GLA_HEADS * GLA_DK), f32),
        'od_norm_g': 1.0 + 0.02 * jax.random.normal(ks[11], (N_ODD, OD_MIX), f32),
        'od_w_out': nrm(ks[12], (N_ODD, OD_MIX, D_MODEL), OD_MIX),
    }


def reference(x, norm_g, w_up, w_down, ev_w_in, ev_ret_norm_g, ev_sinks, ev_w_out,
              od_w_in, od_w_gate_up, od_gate_bias, od_norm_g, od_w_out):
    h = x
    pos = jnp.arange(x.shape[1], dtype=jnp.int32)
    for layer in range(DEPTH):
        i = layer // 2
        u = rms_norm(h, norm_g[layer, 0])
        if layer % 2 == 0:
            u = retention_swa_mixer(u, ev_w_in[i], ev_ret_norm_g[i], ev_sinks[i], ev_w_out[i], pos)
        else:
            u = gla_mixer(u, od_w_in[i], od_w_gate_up[i], od_gate_bias[i], od_norm_g[i], od_w_out[i])
        h = h + rms_norm(u, norm_g[layer, 1])
        u = sq_relu_mlp(rms_norm(h, norm_g[layer, 2]), w_up[layer], w_down[layer])
        h = h + rms_norm(u, norm_g[layer, 3])
    return h
```

```python
import functools

import numpy as np
import jax
import jax.numpy as jnp
from jax import lax
from jax.experimental import pallas as pl
from jax.experimental.pallas import tpu as pltpu

F32 = jnp.float32
BF16 = jnp.bfloat16

D_MODEL = 1024
D_FF = 4 * D_MODEL
NORM_EPS = 1e-6
NEG_INF = -1e30

RET_HEADS = 4
RET_D = 128
RET_CHUNK = 128
ROPE_BASE = 10000.0

SWA_HEADS = 8
SWA_KV_HEADS = 2
SWA_HD = 64
WINDOW = 128

GLA_HEADS = 4
GLA_DK = 128
GLA_DV = 256
GLA_RANK = 16
GLA_TAU = 16.0
GLA_CHUNK = 64

LANES = 128
RANK_PAD = LANES

EV_RQ, EV_RK, EV_RV, EV_RG, EV_AQ, EV_AK, EV_AV, EV_END = 0, 512, 1024, 1536, 2048, 2560, 2688, 2816
OD_Q, OD_K, OD_V, OD_R, OD_END = 0, 512, 1024, 2048, 3072

MIXER_ROWS = 256
MLP_ROWS = 512
FF_CHUNK = 1024
VMEM_LIMIT = 56 * 1024 * 1024

SWA_STACK_HEADS = tuple(h for p in range(4) for h in (p, 4 + p))


def _rms(x, g):
    return x * lax.rsqrt(jnp.mean(x * x, axis=-1, keepdims=True) + NORM_EPS) * g


def _silu(x):
    return x * (1.0 / (1.0 + jnp.exp(-x)))


def _dot(a, b):
    return jnp.dot(a.astype(BF16), b.astype(BF16), preferred_element_type=F32)


def _dot_nt(a, b):
    return lax.dot_general(a.astype(BF16), b.astype(BF16), (((1,), (1,)), ((), ())),
                           preferred_element_type=F32)


def _dot_tn(a, b):
    return lax.dot_general(a.astype(BF16), b.astype(BF16), (((0,), (0,)), ((), ())),
                           preferred_element_type=F32)


def _even_kernel(sinks_ref, h_ref, gpre_ref, win_ref, cos_ref, sin_ref, dec_ref, qdec_ref,
                 kdec_ref, cdec_ref, rng_ref, wout_ref, gpost_ref, o_ref,
                 state_ref, kprev_ref, vprev_ref, mix_ref):
    t = pl.program_id(1)
    rows_total = h_ref.shape[0]

    @pl.when(t == 0)
    def _():
        state_ref[...] = jnp.zeros_like(state_ref)
        kprev_ref[...] = jnp.zeros_like(kprev_ref)
        vprev_ref[...] = jnp.zeros_like(vprev_ref)

    x = h_ref[...]
    u = _rms(x, gpre_ref[...]).astype(BF16)

    def proj(a, b):
        return jnp.dot(u, win_ref[:, a:b], preferred_element_type=F32)

    cos = cos_ref[...]
    sin = sin_ref[...]
    rq = proj(EV_RQ, EV_RK)
    rk = proj(EV_RK, EV_RV)
    rv = proj(EV_RV, EV_RG)
    rg = proj(EV_RG, EV_AQ)
    for hd in range(RET_HEADS):
        cols = slice(hd * RET_D, (hd + 1) * RET_D)
        q = rq[:, cols]
        k = rk[:, cols]
        q = q * cos + pltpu.roll(q, RET_D // 2, 1) * sin
        k = (k * cos + pltpu.roll(k, RET_D // 2, 1) * sin) * (RET_D ** -0.5)
        v = rv[:, cols]
        gate = rg[:, cols]
        for c in range(rows_total // RET_CHUNK):
            rows = slice(c * RET_CHUNK, (c + 1) * RET_CHUNK)
            qc, kc, vc = q[rows], k[rows], v[rows]
            state = state_ref[hd]
            scores = _dot_nt(qc, kc) * dec_ref[hd]
            o = _dot(scores, vc) + _dot(qc * qdec_ref[hd], state)
            state_ref[hd] = cdec_ref[hd] * state + _dot_tn(kc * kdec_ref[hd], vc)
            y = o * lax.rsqrt(jnp.mean(o * o, axis=-1, keepdims=True) + NORM_EPS)
            y = y * rng_ref[:, cols] * _silu(gate[rows])
            mix_ref[rows, cols] = y.astype(BF16)

    aq = proj(EV_AQ, EV_AK) * (SWA_HD ** -0.5)
    ak = proj(EV_AK, EV_AV)
    av = proj(EV_AV, EV_END)
    low = lax.broadcasted_iota(jnp.int32, (WINDOW, LANES), 1) < SWA_HD
    qi = lax.broadcasted_iota(jnp.int32, (WINDOW, 2 * WINDOW), 0)
    ki = lax.broadcasted_iota(jnp.int32, (WINDOW, 2 * WINDOW), 1)
    allowed = (ki > qi) & (ki <= qi + WINDOW)
    first_mask = allowed & (ki >= jnp.where(t > 0, 0, WINDOW))
    for c in range(rows_total // WINDOW):
        rows = slice(c * WINDOW, (c + 1) * WINDOW)
        if c == 0:
            kp, vp = kprev_ref[...], vprev_ref[...]
            mask = first_mask
        else:
            prev = slice((c - 1) * WINDOW, c * WINDOW)
            kp, vp = ak[prev], av[prev]
            mask = allowed
        kk = jnp.concatenate([kp, ak[rows]], axis=0)
        vv = jnp.concatenate([vp, av[rows]], axis=0)
        qs = []
        for p in range(4):
            qg = aq[rows, p * LANES:(p + 1) * LANES]
            qs.append(jnp.where(low, qg, 0.0))
            qs.append(jnp.where(low, 0.0, qg))
        s_all = _dot_nt(jnp.concatenate(qs, axis=0), kk)
        ps = []
        for j, head in enumerate(SWA_STACK_HEADS):
            s = jnp.where(mask, s_all[j * WINDOW:(j + 1) * WINDOW], NEG_INF)
            sink = sinks_ref[head]
            m = jnp.maximum(jnp.max(s, axis=-1, keepdims=True), sink)
            p_un = jnp.exp(s - m)
            denom = jnp.sum(p_un, axis=-1, keepdims=True) + jnp.exp(sink - m)
            ps.append((p_un * (1.0 / denom)).astype(BF16))
        o_all = _dot(jnp.concatenate(ps, axis=0), vv)
        for p in range(4):
            o_lo = o_all[(2 * p) * WINDOW:(2 * p + 1) * WINDOW]
            o_hi = o_all[(2 * p + 1) * WINDOW:(2 * p + 2) * WINDOW]
            mix_ref[rows, 512 + p * LANES:512 + (p + 1) * LANES] = jnp.where(low, o_lo, o_hi).astype(BF16)
    kprev_ref[...] = ak[rows_total - WINDOW:]
    vprev_ref[...] = av[rows_total - WINDOW:]

    out = jnp.dot(mix_ref[...], wout_ref[...], preferred_element_type=F32)
    o_ref[...] = x + _rms(out, gpost_ref[...])


def _even_layer(h, sinks, g_pre, w_in, cos2, sin2, dec, qdec, kdec, cdec, ret_g, w_out, g_post,
                batch, seq):
    nt = seq // MIXER_ROWS
    full2 = lambda b, t: (0, 0)
    full3 = lambda b, t: (0, 0, 0)
    row_spec = pl.BlockSpec((MIXER_ROWS, D_MODEL), lambda b, t: (b * nt + t, 0))
    table_spec = pl.BlockSpec((RET_HEADS, RET_CHUNK, RET_D), full3)
    return pl.pallas_call(
        _even_kernel,
        out_shape=jax.ShapeDtypeStruct(h.shape, h.dtype),
        grid=(batch, nt),
        in_specs=[
            pl.BlockSpec(memory_space=pltpu.SMEM),
            row_spec,
            pl.BlockSpec((1, D_MODEL), full2),
            pl.BlockSpec((D_MODEL, EV_END), full2),
            pl.BlockSpec((MIXER_ROWS, LANES), lambda b, t: (t, 0)),
            pl.BlockSpec((MIXER_ROWS, LANES), lambda b, t: (t, 0)),
            table_spec, table_spec, table_spec, table_spec,
            pl.BlockSpec((1, RET_HEADS * RET_D), full2),
            pl.BlockSpec((D_MODEL, D_MODEL), full2),
            pl.BlockSpec((1, D_MODEL), full2),
        ],
        out_specs=row_spec,
        scratch_shapes=[
            pltpu.VMEM((RET_HEADS, RET_D, RET_D), F32),
            pltpu.VMEM((WINDOW, LANES), F32),
            pltpu.VMEM((WINDOW, LANES), F32),
            pltpu.VMEM((MIXER_ROWS, D_MODEL), BF16),
        ],
        compiler_params=pltpu.CompilerParams(
            dimension_semantics=("arbitrary", "arbitrary"), vmem_limit_bytes=VMEM_LIMIT),
        name="even_mixer",
    )(sinks, h, g_pre, w_in, cos2, sin2, dec, qdec, kdec, cdec, ret_g, w_out, g_post)


def _odd_kernel(h_ref, gpre_ref, win_ref, wlow_ref, wgu_ref, gbias_ref, tri_ref, ng_ref,
                wout_ref, gpost_ref, o_ref, state_ref, mix_ref):
    t = pl.program_id(1)
    rows_total = h_ref.shape[0]

    @pl.when(t == 0)
    def _():
        state_ref[...] = jnp.zeros_like(state_ref)

    x = h_ref[...]
    u = _rms(x, gpre_ref[...]).astype(BF16)

    def proj(a, b):
        return jnp.dot(u, win_ref[:, a:b], preferred_element_type=F32)

    q = proj(OD_Q, OD_K) * (GLA_DK ** -0.5)
    k = proj(OD_K, OD_V)
    v = proj(OD_V, OD_R)
    r = proj(OD_R, OD_END)
    a_low = jnp.dot(u, wlow_ref[...], preferred_element_type=F32)
    gl = _dot(a_low, wgu_ref[...]) + gbias_ref[...]
    log_g = (jnp.minimum(gl, 0.0) - jnp.log1p(jnp.exp(-jnp.abs(gl)))) * (1.0 / GLA_TAU)

    tri = tri_ref[...]
    ri = lax.broadcasted_iota(jnp.int32, (GLA_CHUNK, GLA_CHUNK), 0)
    ci = lax.broadcasted_iota(jnp.int32, (GLA_CHUNK, GLA_CHUNK), 1)
    causal = ri >= ci
    for c in range(rows_total // GLA_CHUNK):
        rows = slice(c * GLA_CHUNK, (c + 1) * GLA_CHUNK)
        lg = log_g[rows]
        lg_hi = lg.astype(BF16)
        lg_lo = (lg - lg_hi.astype(F32)).astype(BF16)
        b = (jnp.dot(tri, lg_hi, preferred_element_type=F32)
             + jnp.dot(tri, lg_lo, preferred_element_type=F32))
        b_last = b[GLA_CHUNK - 1:GLA_CHUNK, :]
        q_in = q[rows] * jnp.exp(b)
        k_in = k[rows] * jnp.exp(-b)
        k_st = k[rows] * jnp.exp(b_last - b)
        decay = jnp.exp(b_last)
        for hd in range(GLA_HEADS):
            kc = slice(hd * GLA_DK, (hd + 1) * GLA_DK)
            vc = slice(hd * GLA_DV, (hd + 1) * GLA_DV)
            qh, vh = q_in[:, kc], v[rows, vc]
            a = jnp.where(causal, _dot_nt(qh, k_in[:, kc]), 0.0)
            state_t = state_ref[hd]
            o = _dot(a, vh) + _dot_nt(qh, state_t)
            state_ref[hd] = decay[:, kc] * state_t + _dot_tn(vh, k_st[:, kc])
            y = o * lax.rsqrt(jnp.mean(o * o, axis=-1, keepdims=True) + NORM_EPS)
            y = y * ng_ref[:, vc] * _silu(r[rows, vc])
            mix_ref[rows, vc] = y.astype(BF16)

    out = jnp.dot(mix_ref[...], wout_ref[...], preferred_element_type=F32)
    o_ref[...] = x + _rms(out, gpost_ref[...])


def _odd_layer(h, g_pre, w_in, w_low, w_gu, g_bias, tri, ng, w_out, g_post, batch, seq):
    nt = seq // MIXER_ROWS
    full2 = lambda b, t: (0, 0)
    row_spec = pl.BlockSpec((MIXER_ROWS, D_MODEL), lambda b, t: (b * nt + t, 0))
    return pl.pallas_call(
        _odd_kernel,
        out_shape=jax.ShapeDtypeStruct(h.shape, h.dtype),
        grid=(batch, nt),
        in_specs=[
            row_spec,
            pl.BlockSpec((1, D_MODEL), full2),
            pl.BlockSpec((D_MODEL, OD_END), full2),
            pl.BlockSpec((D_MODEL, RANK_PAD), full2),
            pl.BlockSpec((RANK_PAD, GLA_HEADS * GLA_DK), full2),
            pl.BlockSpec((1, GLA_HEADS * GLA_DK), full2),
            pl.BlockSpec((GLA_CHUNK, GLA_CHUNK), full2),
            pl.BlockSpec((1, D_MODEL), full2),
            pl.BlockSpec((D_MODEL, D_MODEL), full2),
            pl.BlockSpec((1, D_MODEL), full2),
        ],
        out_specs=row_spec,
        scratch_shapes=[
            pltpu.VMEM((GLA_HEADS, GLA_DV, GLA_DK), F32),
            pltpu.VMEM((MIXER_ROWS, D_MODEL), BF16),
        ],
        compiler_params=pltpu.CompilerParams(
            dimension_semantics=("arbitrary", "arbitrary"), vmem_limit_bytes=VMEM_LIMIT),
        name="odd_mixer",
    )(h, g_pre, w_in, w_low, w_gu, g_bias, tri, ng, w_out, g_post)


def _mlp_kernel(h_ref, gpre_ref, wup_ref, wdown_ref, gpost_ref, o_ref):
    x = h_ref[...]
    u = _rms(x, gpre_ref[...]).astype(BF16)
    acc = jnp.zeros(x.shape, F32)
    for c in range(D_FF // FF_CHUNK):
        cols = slice(c * FF_CHUNK, (c + 1) * FF_CHUNK)
        hid = jnp.maximum(jnp.dot(u, wup_ref[:, cols], preferred_element_type=F32), 0.0)
        acc = acc + jnp.dot((hid * hid).astype(BF16), wdown_ref[cols, :], preferred_element_type=F32)
    o_ref[...] = x + _rms(acc, gpost_ref[...])


def _mlp_layer(h, g_pre, w_up, w_down, g_post):
    n = h.shape[0]
    full2 = lambda i: (0, 0)
    row_spec = pl.BlockSpec((MLP_ROWS, D_MODEL), lambda i: (i, 0))
    return pl.pallas_call(
        _mlp_kernel,
        out_shape=jax.ShapeDtypeStruct(h.shape, h.dtype),
        grid=(n // MLP_ROWS,),
        in_specs=[
            row_spec,
            pl.BlockSpec((1, D_MODEL), full2),
            pl.BlockSpec((D_MODEL, D_FF), full2),
            pl.BlockSpec((D_FF, D_MODEL), full2),
            pl.BlockSpec((1, D_MODEL), full2),
        ],
        out_specs=row_spec,
        compiler_params=pltpu.CompilerParams(
            dimension_semantics=("arbitrary",), vmem_limit_bytes=VMEM_LIMIT),
        name="sq_relu_mlp",
    )(h, g_pre, w_up, w_down, g_post)


def _swa_column_order():
    new = np.arange(SWA_HEADS * SWA_HD)
    p, half, d = new // LANES, (new % LANES) // SWA_HD, new % SWA_HD
    return (half * 4 + p) * SWA_HD + d


def _retention_tables(seq):
    heads = jnp.arange(RET_HEADS, dtype=F32)
    log_gamma = jnp.log1p(-(2.0 ** (-5.0 - heads)))
    idx = jnp.arange(RET_CHUNK, dtype=F32)
    rel = idx[:, None] - idx[None, :]
    causal = rel >= 0
    dec = jnp.where(causal[None], jnp.exp(log_gamma[:, None, None] * jnp.where(causal, rel, 0.0)[None]), 0.0)
    k_dec = jnp.exp(log_gamma[None, :] * (RET_CHUNK - 1 - idx)[:, None])
    q_dec = jnp.exp(log_gamma[None, :] * (idx + 1.0)[:, None])
    c_dec = jnp.exp(log_gamma * RET_CHUNK)
    shape = (RET_HEADS, RET_CHUNK, RET_D)
    kdec = jnp.broadcast_to(k_dec.T[:, :, None], shape)
    qdec = jnp.broadcast_to(q_dec.T[:, :, None], shape)
    cdec = jnp.broadcast_to(c_dec[:, None, None], shape)
    half = RET_D // 2
    inv = ROPE_BASE ** (-jnp.arange(half, dtype=F32) / half)
    ang = jnp.arange(seq, dtype=jnp.int32).astype(F32)[:, None] * inv[None, :]
    cos, sin = jnp.cos(ang), jnp.sin(ang)
    cos2 = jnp.concatenate([cos, cos], axis=-1)
    sin2 = jnp.concatenate([-sin, sin], axis=-1)
    return dec, qdec, kdec, cdec, cos2, sin2


def kernel(x, norm_g, w_up, w_down, ev_w_in, ev_ret_norm_g, ev_sinks, ev_w_out, od_w_in,
           od_w_gate_up, od_gate_bias, od_norm_g, od_w_out):
    batch, seq, d = x.shape
    depth = norm_g.shape[0]
    h = x.reshape(batch * seq, d)
    dec, qdec, kdec, cdec, cos2, sin2 = _retention_tables(seq)
    swa_order = _swa_column_order()
    ev_cols = np.concatenate([np.arange(EV_AQ), EV_AQ + swa_order, np.arange(EV_AK, EV_END)])
    mix_rows = np.concatenate([np.arange(RET_HEADS * RET_D), RET_HEADS * RET_D + swa_order])
    tri = jnp.tril(jnp.ones((GLA_CHUNK, GLA_CHUNK), BF16))
    g = lambda layer, j: norm_g[layer, j].reshape(1, d)
    for layer in range(depth):
        i = layer // 2
        if layer % 2 == 0:
            w_in = ev_w_in[i][:, ev_cols].astype(BF16)
            w_out = ev_w_out[i][mix_rows, :].astype(BF16)
            h = _even_layer(h, ev_sinks[i], g(layer, 0), w_in, cos2, sin2, dec, qdec, kdec, cdec,
                            ev_ret_norm_g[i].reshape(1, -1), w_out, g(layer, 1), batch, seq)
        else:
            w_in = od_w_in[i][:, :OD_END].astype(BF16)
            w_low = jnp.pad(od_w_in[i][:, OD_END:], ((0, 0), (0, RANK_PAD - GLA_RANK))).astype(BF16)
            w_gu = jnp.pad(od_w_gate_up[i], ((0, RANK_PAD - GLA_RANK), (0, 0))).astype(BF16)
            h = _odd_layer(h, g(layer, 0), w_in, w_low, w_gu, od_gate_bias[i].reshape(1, -1), tri,
                           od_norm_g[i].reshape(1, -1), od_w_out[i].astype(BF16), g(layer, 1),
                           batch, seq)
        h = _mlp_layer(h, g(layer, 2), w_up[layer].astype(BF16), w_down[layer].astype(BF16),
                       g(layer, 3))
    return h.reshape(batch, seq, d)
```

```python
import functools

import numpy as np
import jax
import jax.numpy as jnp
from jax import lax
from jax.experimental import pallas as pl
from jax.experimental.pallas import tpu as pltpu

F32 = jnp.float32
BF16 = jnp.bfloat16

D_MODEL = 1024
D_FF = 4 * D_MODEL
NORM_EPS = 1e-6
NEG_INF = -1e30

RET_HEADS = 4
RET_D = 128
ROPE_BASE = 10000.0

SWA_HEADS = 8
SWA_KV_HEADS = 2
SWA_HD = 64
WINDOW = 128

GLA_HEADS = 4
GLA_DK = 128
GLA_DV = 256
GLA_RANK = 16
GLA_TAU = 16.0
GLA_CHUNK = 64

LANES = 128
RANK_PAD = LANES

EV_RQ, EV_RK, EV_RV, EV_RG, EV_AQ, EV_AK, EV_AV, EV_END = 0, 512, 1024, 1536, 2048, 2560, 2688, 2816
OD_Q, OD_K, OD_V, OD_R, OD_END = 0, 512, 1024, 2048, 3072

ROWS = 256
EVEN_FF_CHUNK = 2048
ODD_FF_CHUNK = 2048
SWA_HEADS_PER_MLP_STEP = 4
VMEM_LIMIT = 56 * 1024 * 1024

SWA_STACK_HEADS = tuple(h for p in range(4) for h in (p, 4 + p))


def _rms(x, g):
    return x * lax.rsqrt(jnp.mean(x * x, axis=-1, keepdims=True) + NORM_EPS) * g


def _silu(x):
    return x * (1.0 / (1.0 + jnp.exp(-x)))


def _dot(a, b):
    return jnp.dot(a.astype(BF16), b.astype(BF16), preferred_element_type=F32)


def _dot_nt(a, b):
    return lax.dot_general(a.astype(BF16), b.astype(BF16), (((1,), (1,)), ((), ())),
                           preferred_element_type=F32)


def _dot_tn(a, b):
    return lax.dot_general(a.astype(BF16), b.astype(BF16), (((0,), (0,)), ((), ())),
                           preferred_element_type=F32)


def _mlp_steps(x, gpre_ref, wup_ref, wdown_ref, gpost_ref, o_ref, ff_chunk):
    u = _rms(x, gpre_ref[...]).astype(BF16)
    acc = jnp.zeros(x.shape, F32)
    n_chunks = D_FF // ff_chunk
    for c in range(n_chunks):
        cols = slice(c * ff_chunk, (c + 1) * ff_chunk)
        hid = jnp.maximum(jnp.dot(u, wup_ref[:, cols], preferred_element_type=F32), 0.0)
        hid = (hid * hid).astype(BF16)
        yield
        acc = acc + jnp.dot(hid, wdown_ref[cols, :], preferred_element_type=F32)
        if c == n_chunks - 1:
            o_ref[...] = x + _rms(acc, gpost_ref[...])
        yield


def _even_mixer(mlp, t, x, sinks_ref, gpre_ref, win_ref, cos_ref, sin_ref, dec_ref, qdec_ref, kdec_ref,
                cdec_ref, rng_ref, wout_ref, gpost_ref, state_ref, kprev_ref, vprev_ref, mix_ref):
    rows_total = x.shape[0]
    u = _rms(x, gpre_ref[...]).astype(BF16)

    def proj(a, b):
        return jnp.dot(u, win_ref[:, a:b], preferred_element_type=F32)

    cos = cos_ref[...]
    sin = sin_ref[...]
    rq = proj(EV_RQ, EV_RK)
    rk = proj(EV_RK, EV_RV)
    rv = proj(EV_RV, EV_RG)
    rg = proj(EV_RG, EV_AQ)
    heads = range(RET_HEADS)
    head_cols = [slice(hd * RET_D, (hd + 1) * RET_D) for hd in heads]
    qs = [rq[:, c] * cos + pltpu.roll(rq[:, c], RET_D // 2, 1) * sin for c in head_cols]
    ks = [(rk[:, c] * cos + pltpu.roll(rk[:, c], RET_D // 2, 1) * sin) * (RET_D ** -0.5)
          for c in head_cols]
    vs = [rv[:, c].astype(BF16) for c in head_cols]
    states = [state_ref[hd] for hd in heads]
    scores = [_dot_nt(qs[hd], ks[hd]) * dec_ref[hd] for hd in heads]
    cross = [_dot(qs[hd] * qdec_ref[hd], states[hd]) for hd in heads]
    for hd in heads:
        state_ref[hd] = cdec_ref[hd] * states[hd] + _dot_tn(ks[hd] * kdec_ref[hd], vs[hd])
    outs = [_dot(scores[hd], vs[hd]) + cross[hd] for hd in heads]
    for hd in heads:
        o = outs[hd]
        y = o * lax.rsqrt(jnp.mean(o * o, axis=-1, keepdims=True) + NORM_EPS)
        mix_ref[:, head_cols[hd]] = (y * rng_ref[:, head_cols[hd]] * _silu(rg[:, head_cols[hd]])).astype(BF16)

    aq = proj(EV_AQ, EV_AK) * (SWA_HD ** -0.5)
    ak = proj(EV_AK, EV_AV)
    av = proj(EV_AV, EV_END)
    low = lax.broadcasted_iota(jnp.int32, (WINDOW, LANES), 1) < SWA_HD
    qi = lax.broadcasted_iota(jnp.int32, (WINDOW, 2 * WINDOW), 0)
    ki = lax.broadcasted_iota(jnp.int32, (WINDOW, 2 * WINDOW), 1)
    allowed = (ki > qi) & (ki <= qi + WINDOW)
    first_mask = allowed & (ki >= jnp.where(t > 0, 0, WINDOW))
    for c in range(rows_total // WINDOW):
        rows = slice(c * WINDOW, (c + 1) * WINDOW)
        if c == 0:
            kp, vp = kprev_ref[...], vprev_ref[...]
            mask = first_mask
        else:
            prev = slice((c - 1) * WINDOW, c * WINDOW)
            kp, vp = ak[prev], av[prev]
            mask = allowed
        kk = jnp.concatenate([kp, ak[rows]], axis=0)
        vv = jnp.concatenate([vp, av[rows]], axis=0)
        qs = []
        for p in range(4):
            qg = aq[rows, p * LANES:(p + 1) * LANES]
            qs.append(jnp.where(low, qg, 0.0))
            qs.append(jnp.where(low, 0.0, qg))
        s_all = _dot_nt(jnp.concatenate(qs, axis=0), kk)
        ps = []
        for j, head in enumerate(SWA_STACK_HEADS):
            s = jnp.where(mask, s_all[j * WINDOW:(j + 1) * WINDOW], NEG_INF)
            sink = sinks_ref[head]
            m = jnp.maximum(jnp.max(s, axis=-1, keepdims=True), sink)
            p_un = jnp.exp(s - m)
            denom = jnp.sum(p_un, axis=-1, keepdims=True) + jnp.exp(sink - m)
            ps.append((p_un * (1.0 / denom)).astype(BF16))
            if j % SWA_HEADS_PER_MLP_STEP == SWA_HEADS_PER_MLP_STEP - 1:
                next(mlp, None)
        o_all = _dot(jnp.concatenate(ps, axis=0), vv)
        for p in range(4):
            o_lo = o_all[(2 * p) * WINDOW:(2 * p + 1) * WINDOW]
            o_hi = o_all[(2 * p + 1) * WINDOW:(2 * p + 2) * WINDOW]
            mix_ref[rows, 512 + p * LANES:512 + (p + 1) * LANES] = jnp.where(low, o_lo, o_hi).astype(BF16)
    kprev_ref[...] = ak[rows_total - WINDOW:]
    vprev_ref[...] = av[rows_total - WINDOW:]
    for _ in mlp:
        pass

    out = jnp.dot(mix_ref[...], wout_ref[...], preferred_element_type=F32)
    return x + _rms(out, gpost_ref[...])


def _even_kernel(blocks_per_seq, sinks_ref, h_ref, gpre_ref, win_ref, cos_ref, sin_ref, dec_ref,
                 qdec_ref, kdec_ref, cdec_ref, rng_ref, wout_ref, gpost_ref,
                 gmlp_ref, wup_ref, wdown_ref, gmlp_post_ref, o_ref,
                 state_ref, kprev_ref, vprev_ref, mix_ref, h1_ref):
    step = pl.program_id(0)
    t = step % blocks_per_seq

    @pl.when(step == 0)
    def _():
        h1_ref[...] = jnp.zeros_like(h1_ref)

    @pl.when(t == 0)
    def _():
        state_ref[...] = jnp.zeros_like(state_ref)
        kprev_ref[...] = jnp.zeros_like(kprev_ref)
        vprev_ref[...] = jnp.zeros_like(vprev_ref)

    mlp = _mlp_steps(h1_ref[(step + 1) % 2], gmlp_ref, wup_ref, wdown_ref, gmlp_post_ref, o_ref,
                     EVEN_FF_CHUNK)
    h1_ref[step % 2] = _even_mixer(
        mlp, t, h_ref[...], sinks_ref, gpre_ref, win_ref, cos_ref, sin_ref, dec_ref, qdec_ref, kdec_ref,
        cdec_ref, rng_ref, wout_ref, gpost_ref, state_ref, kprev_ref, vprev_ref, mix_ref)


def _resident(shape):
    zeros = (0,) * len(shape)
    return pl.BlockSpec(shape, lambda s: zeros, pipeline_mode=pl.Buffered(1))


def _row_specs(n_blocks):
    in_spec = pl.BlockSpec((ROWS, D_MODEL), lambda s: (jnp.minimum(s, n_blocks - 1), 0))
    out_spec = pl.BlockSpec((ROWS, D_MODEL), lambda s: (jnp.maximum(s - 1, 0), 0))
    return in_spec, out_spec


def _even_layer(h, sinks, g_pre, w_in, cos2, sin2, dec, qdec, kdec, cdec, ret_g, w_out, g_post,
                g_mlp, w_up, w_down, g_mlp_post, seq):
    nt = seq // ROWS
    n_blocks = h.shape[0] // ROWS
    in_spec, out_spec = _row_specs(n_blocks)
    pos_spec = pl.BlockSpec((ROWS, LANES), lambda s: (jnp.minimum(s, n_blocks - 1) % nt, 0))
    return pl.pallas_call(
        functools.partial(_even_kernel, nt),
        out_shape=jax.ShapeDtypeStruct(h.shape, h.dtype),
        grid=(n_blocks + 1,),
        in_specs=[
            pl.BlockSpec(memory_space=pltpu.SMEM),
            in_spec,
            _resident((1, D_MODEL)),
            _resident((D_MODEL, EV_END)),
            pos_spec, pos_spec,
            _resident((RET_HEADS, ROWS, ROWS)),
            _resident((RET_HEADS, ROWS, RET_D)),
            _resident((RET_HEADS, ROWS, RET_D)),
            _resident((RET_HEADS, RET_D, RET_D)),
            _resident((1, RET_HEADS * RET_D)),
            _resident((D_MODEL, D_MODEL)),
            _resident((1, D_MODEL)),
            _resident((1, D_MODEL)),
            _resident((D_MODEL, D_FF)),
            _resident((D_FF, D_MODEL)),
            _resident((1, D_MODEL)),
        ],
        out_specs=out_spec,
        scratch_shapes=[
            pltpu.VMEM((RET_HEADS, RET_D, RET_D), F32),
            pltpu.VMEM((WINDOW, LANES), F32),
            pltpu.VMEM((WINDOW, LANES), F32),
            pltpu.VMEM((ROWS, D_MODEL), BF16),
            pltpu.VMEM((2, ROWS, D_MODEL), F32),
        ],
        compiler_params=pltpu.CompilerParams(
            dimension_semantics=("arbitrary",), vmem_limit_bytes=VMEM_LIMIT),
        name="even_layer",
    )(sinks, h, g_pre, w_in, cos2, sin2, dec, qdec, kdec, cdec, ret_g, w_out, g_post,
      g_mlp, w_up, w_down, g_mlp_post)


def _odd_mixer(mlp, x, gpre_ref, win_ref, wlow_ref, wgu_ref, gbias_ref, tri_ref, ng_ref, wout_ref,
               gpost_ref, state_ref, mix_ref):
    rows_total = x.shape[0]
    u = _rms(x, gpre_ref[...]).astype(BF16)

    def proj(a, b):
        return jnp.dot(u, win_ref[:, a:b], preferred_element_type=F32)

    q = proj(OD_Q, OD_K) * (GLA_DK ** -0.5)
    k = proj(OD_K, OD_V)
    v = proj(OD_V, OD_R).astype(BF16)
    r = proj(OD_R, OD_END)
    a_low = jnp.dot(u, wlow_ref[...], preferred_element_type=F32)
    gl = _dot(a_low, wgu_ref[...]) + gbias_ref[...]
    next(mlp, None)
    log_g =(jnp.minimum(gl, 0.0) - jnp.log1p(jnp.exp(-jnp.abs(gl)))) * (1.0 / GLA_TAU)

    n_chunks = rows_total // GLA_CHUNK
    chunk_rows = [slice(c * GLA_CHUNK, (c + 1) * GLA_CHUNK) for c in range(n_chunks)]
    lg_hi = log_g.astype(BF16)
    lg_lo = (log_g - lg_hi.astype(F32)).astype(BF16)
    tri = tri_ref[...]
    b = (jnp.dot(tri, lg_hi, preferred_element_type=F32)
         + jnp.dot(tri, lg_lo, preferred_element_type=F32))
    b_last = [b[rows.stop - 1:rows.stop, :] for rows in chunk_rows]
    b_last_full = jnp.concatenate(
        [jnp.broadcast_to(bl, (GLA_CHUNK, bl.shape[1])) for bl in b_last], axis=0)
    q_in = q * jnp.exp(b)
    k_in = k * jnp.exp(-b)
    k_st = k * jnp.exp(b_last_full - b)
    next(mlp, None)

    ri = lax.broadcasted_iota(jnp.int32, (rows_total, rows_total), 0)
    ci = lax.broadcasted_iota(jnp.int32, (rows_total, rows_total), 1)
    causal = (ci <= ri) & (ci >= ri - (ri & (GLA_CHUNK - 1)))
    row8 = lax.broadcasted_iota(jnp.int32, (8, GLA_DK), 0)
    heads = range(GLA_HEADS)
    kcs = [slice(hd * GLA_DK, (hd + 1) * GLA_DK) for hd in heads]
    vcs = [slice(hd * GLA_DV, (hd + 1) * GLA_DV) for hd in heads]
    scores = [jnp.where(causal, _dot_nt(q_in[:, kcs[hd]], k_in[:, kcs[hd]]), 0.0) for hd in heads]
    kvs = [[_dot_tn(k_st[rows, kcs[hd]], v[rows, vcs[hd]]) for rows in chunk_rows] for hd in heads]
    crosses = []
    for hd in heads:
        tile = jnp.zeros((8, GLA_DK), F32)
        for c in range(n_chunks):
            tile = jnp.where(row8 == c, b_last[c][:, kcs[hd]], tile)
        tile = jnp.concatenate([tile, jnp.zeros((GLA_DK - 8, GLA_DK), F32)], axis=0).T
        state = state_ref[hd]
        cross = []
        for c in range(n_chunks):
            cross.append(_dot(q_in[chunk_rows[c], kcs[hd]], state))
            state = jnp.exp(tile[:, c:c + 1]) * state + kvs[hd][c]
        state_ref[hd] = state
        crosses.append(jnp.concatenate(cross, axis=0))
    outs = [_dot(scores[hd], v[:, vcs[hd]]) + crosses[hd] for hd in heads]
    for hd in heads:
        o = outs[hd]
        y = o * lax.rsqrt(jnp.mean(o * o, axis=-1, keepdims=True) + NORM_EPS)
        mix_ref[:, vcs[hd]] = (y * ng_ref[:, vcs[hd]] * _silu(r[:, vcs[hd]])).astype(BF16)
    for _ in mlp:
        pass

    out = jnp.dot(mix_ref[...], wout_ref[...], preferred_element_type=F32)
    return x + _rms(out, gpost_ref[...])


def _odd_kernel(blocks_per_seq, h_ref, gpre_ref, win_ref, wlow_ref, wgu_ref, gbias_ref, tri_ref,
                ng_ref, wout_ref, gpost_ref, gmlp_ref, wup_ref, wdown_ref, gmlp_post_ref, o_ref,
                state_ref, mix_ref, h1_ref):
    step = pl.program_id(0)

    @pl.when(step == 0)
    def _():
        h1_ref[...] = jnp.zeros_like(h1_ref)

    @pl.when(step % blocks_per_seq == 0)
    def _():
        state_ref[...] = jnp.zeros_like(state_ref)

    mlp = _mlp_steps(h1_ref[(step + 1) % 2], gmlp_ref, wup_ref, wdown_ref, gmlp_post_ref, o_ref,
                     ODD_FF_CHUNK)
    h1_ref[step % 2] = _odd_mixer(
        mlp, h_ref[...], gpre_ref, win_ref, wlow_ref, wgu_ref, gbias_ref, tri_ref, ng_ref, wout_ref,
        gpost_ref, state_ref, mix_ref)


def _odd_layer(h, g_pre, w_in, w_low, w_gu, g_bias, tri, ng, w_out, g_post,
               g_mlp, w_up, w_down, g_mlp_post, seq):
    nt = seq // ROWS
    n_blocks = h.shape[0] // ROWS
    in_spec, out_spec = _row_specs(n_blocks)
    return pl.pallas_call(
        functools.partial(_odd_kernel, nt),
        out_shape=jax.ShapeDtypeStruct(h.shape, h.dtype),
        grid=(n_blocks + 1,),
        in_specs=[
            in_spec,
            _resident((1, D_MODEL)),
            _resident((D_MODEL, OD_END)),
            _resident((D_MODEL, RANK_PAD)),
            _resident((RANK_PAD, GLA_HEADS * GLA_DK)),
            _resident((1, GLA_HEADS * GLA_DK)),
            _resident((ROWS, ROWS)),
            _resident((1, D_MODEL)),
            _resident((D_MODEL, D_MODEL)),
            _resident((1, D_MODEL)),
            _resident((1, D_MODEL)),
            _resident((D_MODEL, D_FF)),
            _resident((D_FF, D_MODEL)),
            _resident((1, D_MODEL)),
        ],
        out_specs=out_spec,
        scratch_shapes=[
            pltpu.VMEM((GLA_HEADS, GLA_DK, GLA_DV), F32),
            pltpu.VMEM((ROWS, D_MODEL), BF16),
            pltpu.VMEM((2, ROWS, D_MODEL), F32),
        ],
        compiler_params=pltpu.CompilerParams(
            dimension_semantics=("arbitrary",), vmem_limit_bytes=VMEM_LIMIT),
        name="odd_layer",
    )(h, g_pre, w_in, w_low, w_gu, g_bias, tri, ng, w_out, g_post, g_mlp, w_up, w_down, g_mlp_post)


def _swa_column_order():
    new = np.arange(SWA_HEADS * SWA_HD)
    p, half, d = new // LANES, (new % LANES) // SWA_HD, new % SWA_HD
    return (half * 4 + p) * SWA_HD + d


def _retention_tables(seq):
    heads = jnp.arange(RET_HEADS, dtype=F32)
    log_gamma = jnp.log1p(-(2.0 ** (-5.0 - heads)))
    idx = jnp.arange(ROWS, dtype=F32)
    rel = idx[:, None] - idx[None, :]
    causal = rel >= 0
    dec = jnp.where(causal[None], jnp.exp(log_gamma[:, None, None] * jnp.where(causal, rel, 0.0)[None]), 0.0)
    k_dec = jnp.exp(log_gamma[None, :] * (ROWS - 1 - idx)[:, None])
    q_dec = jnp.exp(log_gamma[None, :] * (idx + 1.0)[:, None])
    c_dec = jnp.exp(log_gamma * ROWS)
    shape = (RET_HEADS, ROWS, RET_D)
    kdec = jnp.broadcast_to(k_dec.T[:, :, None], shape)
    qdec = jnp.broadcast_to(q_dec.T[:, :, None], shape)
    cdec = jnp.broadcast_to(c_dec[:, None, None], (RET_HEADS, RET_D, RET_D))
    half = RET_D // 2
    inv = ROPE_BASE ** (-jnp.arange(half, dtype=F32) / half)
    ang = jnp.arange(seq, dtype=jnp.int32).astype(F32)[:, None] * inv[None, :]
    cos, sin = jnp.cos(ang), jnp.sin(ang)
    cos2 = jnp.concatenate([cos, cos], axis=-1)
    sin2 = jnp.concatenate([-sin, sin], axis=-1)
    return dec, qdec, kdec, cdec, cos2, sin2


def kernel(x, norm_g, w_up, w_down, ev_w_in, ev_ret_norm_g, ev_sinks, ev_w_out, od_w_in,
           od_w_gate_up, od_gate_bias, od_norm_g, od_w_out):
    batch, seq, d = x.shape
    depth = norm_g.shape[0]
    h = x.reshape(batch * seq, d)
    dec, qdec, kdec, cdec, cos2, sin2 = _retention_tables(seq)
    swa_order = _swa_column_order()
    ev_cols = np.concatenate([np.arange(EV_AQ), EV_AQ + swa_order, np.arange(EV_AK, EV_END)])
    mix_rows = np.concatenate([np.arange(RET_HEADS * RET_D), RET_HEADS * RET_D + swa_order])
    pos = np.arange(ROWS)
    tri = jnp.asarray((pos[None, :] <= pos[:, None])
                      & (pos[None, :] // GLA_CHUNK == pos[:, None] // GLA_CHUNK), BF16)
    g = lambda layer, j: norm_g[layer, j].reshape(1, d)
    for layer in range(depth):
        i = layer // 2
        mlp_args = (g(layer, 2), w_up[layer].astype(BF16), w_down[layer].astype(BF16), g(layer, 3))
        if layer % 2 == 0:
            w_in = ev_w_in[i][:, ev_cols].astype(BF16)
            w_out = ev_w_out[i][mix_rows, :].astype(BF16)
            h = _even_layer(h, ev_sinks[i], g(layer, 0), w_in, cos2, sin2, dec, qdec, kdec, cdec,
                            ev_ret_norm_g[i].reshape(1, -1), w_out, g(layer, 1), *mlp_args, seq)
        else:
            w_in = od_w_in[i][:, :OD_END].astype(BF16)
            w_low = jnp.pad(od_w_in[i][:, OD_END:], ((0, 0), (0, RANK_PAD - GLA_RANK))).astype(BF16)
            w_gu = jnp.pad(od_w_gate_up[i], ((0, RANK_PAD - GLA_RANK), (0, 0))).astype(BF16)
            h = _odd_layer(h, g(layer, 0), w_in, w_low, w_gu, od_gate_bias[i].reshape(1, -1), tri,
                           od_norm_g[i].reshape(1, -1), od_w_out[i].astype(BF16), g(layer, 1),
                           *mlp_args, seq)
    return h.reshape(batch, seq, d)
```

```python
import functools

import numpy as np
import jax
import jax.numpy as jnp
from jax import lax
from jax.experimental import pallas as pl
from jax.experimental.pallas import tpu as pltpu

F32 = jnp.float32
BF16 = jnp.bfloat16

D_MODEL = 1024
D_FF = 4 * D_MODEL
NORM_EPS = 1e-6
NEG_INF = -1e30

RET_HEADS = 4
RET_D = 128
ROPE_BASE = 10000.0

SWA_HEADS = 8
SWA_KV_HEADS = 2
SWA_GROUP = SWA_HEADS // SWA_KV_HEADS
SWA_HD = 64
WINDOW = 128

GLA_HEADS = 4
GLA_DK = 128
GLA_DV = 256
GLA_RANK = 16
GLA_TAU = 16.0
GLA_CHUNK = 64

LANES = 128
SUBLANES = 8
RANK_PAD = LANES

EV_RQ, EV_RK, EV_RV, EV_RG, EV_AQ, EV_AK, EV_AV, EV_END = 0, 512, 1024, 1536, 2048, 2560, 2688, 2816
EV_RET_MIX = RET_HEADS * RET_D
OD_Q, OD_K, OD_V, OD_R, OD_END = 0, 512, 1024, 2048, 3072

ROWS = 256
EVEN_FF_CHUNK = 2048
ODD_FF_CHUNK = 2048
SWA_HEADS_PER_MLP_STEP = 4
EVEN_OUT_PIECES = 1
ODD_OUT_PIECES = 2
VMEM_LIMIT = 56 * 1024 * 1024

SWA_STACK_HEADS = tuple(h for p in range(SWA_GROUP) for h in (p, SWA_GROUP + p))


def _rms(x, g):
    return x * lax.rsqrt(jnp.mean(x * x, axis=-1, keepdims=True) + NORM_EPS) * g


def _silu(x):
    return x * (1.0 / (1.0 + jnp.exp(-x)))


def _dot(a, b):
    return jnp.dot(a.astype(BF16), b.astype(BF16), preferred_element_type=F32)


def _dot_nt(a, b):
    return lax.dot_general(a.astype(BF16), b.astype(BF16), (((1,), (1,)), ((), ())),
                           preferred_element_type=F32)


def _dot_tn(a, b):
    return lax.dot_general(a.astype(BF16), b.astype(BF16), (((0,), (0,)), ((), ())),
                           preferred_element_type=F32)


def _mlp_steps(x, gpre_ref, wup_ref, wdown_ref, gpost_ref, o_ref, ff_chunk):
    u = _rms(x, gpre_ref[...]).astype(BF16)
    yield
    acc = jnp.zeros(x.shape, F32)
    n_chunks = D_FF // ff_chunk
    for c in range(n_chunks):
        cols = slice(c * ff_chunk, (c + 1) * ff_chunk)
        hid = jnp.maximum(jnp.dot(u, wup_ref[:, cols], preferred_element_type=F32), 0.0)
        hid = (hid * hid).astype(BF16)
        yield
        acc = acc + jnp.dot(hid, wdown_ref[cols, :], preferred_element_type=F32)
        if c == n_chunks - 1:
            o_ref[...] = x + _rms(acc, gpost_ref[...])
        yield


def _out_proj(x, mix_ref, wout_ref, gpost_ref, n_pieces):
    pieces = []
    for p in range(n_pieces):
        rows = slice(p * ROWS // n_pieces, (p + 1) * ROWS // n_pieces)
        out = jnp.dot(mix_ref[rows, :], wout_ref[...], preferred_element_type=F32)
        pieces.append(x[rows] + _rms(out, gpost_ref[...]))
    return jnp.concatenate(pieces, axis=0)


def _even_mixer(mlp, t, x, sinks_ref, gpre_ref, win_ref, cos_ref, sin_ref, dec_ref, qdec_ref, kdec_ref,
                cdec_ref, rng_ref, wout_ref, gpost_ref, state_ref, kprev_ref, vprev_ref, mix_ref):
    u = _rms(x, gpre_ref[...]).astype(BF16)
    next(mlp)

    def proj(a, b):
        return jnp.dot(u, win_ref[:, a:b], preferred_element_type=F32)

    cos = cos_ref[...]
    sin = sin_ref[...]
    rq = proj(EV_RQ, EV_RK)
    rk = proj(EV_RK, EV_RV)
    rv = proj(EV_RV, EV_RG)
    rg = proj(EV_RG, EV_AQ)
    heads = range(RET_HEADS)
    head_cols = [slice(hd * RET_D, (hd + 1) * RET_D) for hd in heads]
    qs = [rq[:, c] * cos + pltpu.roll(rq[:, c], RET_D // 2, 1) * sin for c in head_cols]
    ks = [(rk[:, c] * cos + pltpu.roll(rk[:, c], RET_D // 2, 1) * sin) * (RET_D ** -0.5)
          for c in head_cols]
    vs = [rv[:, c].astype(BF16) for c in head_cols]
    states = [state_ref[hd] for hd in heads]
    scores = [_dot_nt(qs[hd], ks[hd]) * dec_ref[hd] for hd in heads]
    cross = [_dot(qs[hd] * qdec_ref[hd], states[hd]) for hd in heads]
    for hd in heads:
        state_ref[hd] = cdec_ref[hd] * states[hd] + _dot_tn(ks[hd] * kdec_ref[hd], vs[hd])
    outs = [_dot(scores[hd], vs[hd]) + cross[hd] for hd in heads]
    for hd in heads:
        o = outs[hd]
        y = o * lax.rsqrt(jnp.mean(o * o, axis=-1, keepdims=True) + NORM_EPS)
        mix_ref[:, head_cols[hd]] = (y * rng_ref[:, head_cols[hd]] * _silu(rg[:, head_cols[hd]])).astype(BF16)

    aq = proj(EV_AQ, EV_AK) * (SWA_HD ** -0.5)
    ak = proj(EV_AK, EV_AV)
    av = proj(EV_AV, EV_END)
    low = lax.broadcasted_iota(jnp.int32, (WINDOW, LANES), 1) < SWA_HD
    qi =lax.broadcasted_iota(jnp.int32, (WINDOW, 2 * WINDOW), 0)
    ki = lax.broadcasted_iota(jnp.int32, (WINDOW, 2 * WINDOW), 1)
    allowed = (ki > qi) & (ki <= qi + WINDOW)
    first_mask = allowed & (ki >= jnp.where(t > 0, 0, WINDOW))
    for c in range(ROWS // WINDOW):
        rows = slice(c * WINDOW, (c + 1) * WINDOW)
        if c == 0:
            kp, vp = kprev_ref[...], vprev_ref[...]
            mask = first_mask
        else:
            prev = slice((c - 1) * WINDOW, c * WINDOW)
            kp, vp = ak[prev], av[prev]
            mask = allowed
        kk = jnp.concatenate([kp, ak[rows]], axis=0)
        vv = jnp.concatenate([vp, av[rows]], axis=0)
        q_parts = []
        for p in range(SWA_GROUP):
            qg = aq[rows, p * LANES:(p + 1) * LANES]
            q_parts.append(jnp.where(low, qg, 0.0))
            q_parts.append(jnp.where(low, 0.0, qg))
        s_all = _dot_nt(jnp.concatenate(q_parts, axis=0), kk)
        ps = []
        for j, head in enumerate(SWA_STACK_HEADS):
            s = jnp.where(mask, s_all[j * WINDOW:(j + 1) * WINDOW], NEG_INF)
            sink = sinks_ref[head]
            m = jnp.maximum(jnp.max(s, axis=-1, keepdims=True), sink)
            p_un = jnp.exp(s - m)
            denom = jnp.sum(p_un, axis=-1, keepdims=True) + jnp.exp(sink - m)
            ps.append((p_un * (1.0 / denom)).astype(BF16))
            if j % SWA_HEADS_PER_MLP_STEP == SWA_HEADS_PER_MLP_STEP - 1:
                next(mlp, None)
        o_all = _dot(jnp.concatenate(ps, axis=0), vv)
        for p in range(SWA_GROUP):
            o_lo = o_all[(2 * p) * WINDOW:(2 * p + 1) * WINDOW]
            o_hi = o_all[(2 * p + 1) * WINDOW:(2 * p + 2) * WINDOW]
            mix_ref[rows, EV_RET_MIX + p * LANES:EV_RET_MIX + (p + 1) * LANES] = (
                jnp.where(low, o_lo, o_hi).astype(BF16))
    kprev_ref[...] = ak[ROWS - WINDOW:]
    vprev_ref[...] = av[ROWS - WINDOW:]
    for _ in mlp:
        pass

    return _out_proj(x, mix_ref, wout_ref, gpost_ref, EVEN_OUT_PIECES)


def _even_kernel(blocks_per_seq, sinks_ref, h_ref, gpre_ref, win_ref, cos_ref, sin_ref, dec_ref,
                 qdec_ref, kdec_ref, cdec_ref, rng_ref, wout_ref, gpost_ref,
                 gmlp_ref, wup_ref, wdown_ref, gmlp_post_ref, o_ref,
                 state_ref, kprev_ref, vprev_ref, mix_ref, h1_ref):
    step = pl.program_id(0)
    t = step % blocks_per_seq

    @pl.when(step == 0)
    def _():
        h1_ref[...] = jnp.zeros_like(h1_ref)

    @pl.when(t == 0)
    def _():
        state_ref[...] = jnp.zeros_like(state_ref)
        kprev_ref[...] = jnp.zeros_like(kprev_ref)
        vprev_ref[...] = jnp.zeros_like(vprev_ref)

    mlp = _mlp_steps(h1_ref[(step + 1) % 2], gmlp_ref, wup_ref, wdown_ref, gmlp_post_ref, o_ref,
                     EVEN_FF_CHUNK)
    h1_ref[step % 2] = _even_mixer(
        mlp, t, h_ref[...], sinks_ref, gpre_ref, win_ref, cos_ref, sin_ref, dec_ref, qdec_ref, kdec_ref,
        cdec_ref, rng_ref, wout_ref, gpost_ref, state_ref, kprev_ref, vprev_ref, mix_ref)


def _resident(shape):
    zeros = (0,) * len(shape)
    return pl.BlockSpec(shape, lambda s: zeros, pipeline_mode=pl.Buffered(1))


def _row_specs(n_blocks):
    in_spec = pl.BlockSpec((ROWS, D_MODEL), lambda s: (jnp.minimum(s, n_blocks - 1), 0))
    out_spec = pl.BlockSpec((ROWS, D_MODEL), lambda s: (jnp.maximum(s - 1, 0), 0))
    return in_spec, out_spec


def _even_layer(h, sinks, g_pre, w_in, cos2, sin2, dec, qdec, kdec, cdec, ret_g, w_out, g_post,
                g_mlp, w_up, w_down, g_mlp_post, seq):
    nt = seq // ROWS
    n_blocks = h.shape[0] // ROWS
    in_spec, out_spec = _row_specs(n_blocks)
    pos_spec = pl.BlockSpec((ROWS, LANES), lambda s: (jnp.minimum(s, n_blocks - 1) % nt, 0))
    return pl.pallas_call(
        functools.partial(_even_kernel, nt),
        out_shape=jax.ShapeDtypeStruct(h.shape, h.dtype),
        grid=(n_blocks + 1,),
        in_specs=[
            pl.BlockSpec(memory_space=pltpu.SMEM),
            in_spec,
            _resident((1, D_MODEL)),
            _resident((D_MODEL, EV_END)),
            pos_spec, pos_spec,
            _resident((RET_HEADS, ROWS, ROWS)),
            _resident((RET_HEADS, ROWS, RET_D)),
            _resident((RET_HEADS, ROWS, RET_D)),
            _resident((RET_HEADS, RET_D, RET_D)),
            _resident((1, RET_HEADS * RET_D)),
            _resident((D_MODEL, D_MODEL)),
            _resident((1, D_MODEL)),
            _resident((1, D_MODEL)),
            _resident((D_MODEL, D_FF)),
            _resident((D_FF, D_MODEL)),
            _resident((1, D_MODEL)),
        ],
        out_specs=out_spec,
        scratch_shapes=[
            pltpu.VMEM((RET_HEADS, RET_D, RET_D), F32),
            pltpu.VMEM((WINDOW, LANES), F32),
            pltpu.VMEM((WINDOW, LANES), F32),
            pltpu.VMEM((ROWS, D_MODEL), BF16),
            pltpu.VMEM((2, ROWS, D_MODEL), F32),
        ],
        compiler_params=pltpu.CompilerParams(
            dimension_semantics=("arbitrary",), vmem_limit_bytes=VMEM_LIMIT),
        name="even_layer",
    )(sinks, h, g_pre, w_in, cos2, sin2, dec, qdec, kdec, cdec, ret_g, w_out, g_post,
      g_mlp, w_up, w_down, g_mlp_post)


def _odd_mixer(mlp, x, gpre_ref, win_ref, wlow_ref, wgu_ref, gbias_ref, tri_ref, ng_ref, wout_ref,
               gpost_ref, state_ref, mix_ref):
    u = _rms(x, gpre_ref[...]).astype(BF16)
    next(mlp)

    def proj(a, b):
        return jnp.dot(u, win_ref[:, a:b], preferred_element_type=F32)

    a_low = jnp.dot(u, wlow_ref[...], preferred_element_type=F32)
    k = proj(OD_K, OD_V)
    gl = _dot(a_low, wgu_ref[...]) + gbias_ref[...]
    q = proj(OD_Q, OD_K) * (GLA_DK ** -0.5)
    log_g = (jnp.minimum(gl, 0.0) - jnp.log1p(jnp.exp(-jnp.abs(gl)))) * (1.0 / GLA_TAU)

    n_chunks = ROWS // GLA_CHUNK
    chunk_rows = [slice(c * GLA_CHUNK, (c + 1) * GLA_CHUNK) for c in range(n_chunks)]
    lg_hi = log_g.astype(BF16)
    lg_lo = (log_g - lg_hi.astype(F32)).astype(BF16)
    v = proj(OD_V, OD_R).astype(BF16)
    tri = tri_ref[...]
    b = (jnp.dot(tri, lg_hi, preferred_element_type=F32)
         + jnp.dot(tri, lg_lo, preferred_element_type=F32))
    r = proj(OD_R, OD_END)
    next(mlp, None)
    b_last =[b[rows.stop - 1:rows.stop, :] for rows in chunk_rows]
    b_last_full = jnp.concatenate(
        [jnp.broadcast_to(bl, (GLA_CHUNK, bl.shape[1])) for bl in b_last], axis=0)
    q_in = q * jnp.exp(b)
    k_in = k * jnp.exp(-b)
    k_st = k * jnp.exp(b_last_full - b)

    ri = lax.broadcasted_iota(jnp.int32, (ROWS, ROWS), 0)
    ci = lax.broadcasted_iota(jnp.int32, (ROWS, ROWS), 1)
    causal = (ci <= ri) & (ci >= ri - (ri & (GLA_CHUNK - 1)))
    row_id = lax.broadcasted_iota(jnp.int32, (SUBLANES, GLA_DK), 0)
    heads = range(GLA_HEADS)
    kcs = [slice(hd * GLA_DK, (hd + 1) * GLA_DK) for hd in heads]
    vcs = [slice(hd * GLA_DV, (hd + 1) * GLA_DV) for hd in heads]
    scores = [jnp.where(causal, _dot_nt(q_in[:, kcs[hd]], k_in[:, kcs[hd]]), 0.0) for hd in heads]
    kvs = [[_dot_tn(k_st[rows, kcs[hd]], v[rows, vcs[hd]]) for rows in chunk_rows] for hd in heads]
    crosses = []
    for hd in heads:
        tile = jnp.zeros((SUBLANES, GLA_DK), F32)
        for c in range(n_chunks):
            tile = jnp.where(row_id == c, b_last[c][:, kcs[hd]], tile)
        tile = jnp.concatenate([tile, jnp.zeros((GLA_DK - SUBLANES, GLA_DK), F32)], axis=0).T
        state = state_ref[hd]
        cross = []
        for c in range(n_chunks):
            cross.append(_dot(q_in[chunk_rows[c], kcs[hd]], state))
            state = jnp.exp(tile[:, c:c + 1]) * state + kvs[hd][c]
        state_ref[hd] = state
        crosses.append(jnp.concatenate(cross, axis=0))
    outs = [_dot(scores[hd], v[:, vcs[hd]]) + crosses[hd] for hd in heads]
    next(mlp, None)
    for hd in heads:
        o = outs[hd]
        y = o * lax.rsqrt(jnp.mean(o * o, axis=-1, keepdims=True) + NORM_EPS)
        mix_ref[:, vcs[hd]] = (y * ng_ref[:, vcs[hd]] * _silu(r[:, vcs[hd]])).astype(BF16)
    for _ in mlp:
        pass

    return _out_proj(x, mix_ref, wout_ref, gpost_ref, ODD_OUT_PIECES)


def _odd_kernel(blocks_per_seq, h_ref, gpre_ref, win_ref, wlow_ref, wgu_ref, gbias_ref, tri_ref,
                ng_ref, wout_ref, gpost_ref, gmlp_ref, wup_ref, wdown_ref, gmlp_post_ref, o_ref,
                state_ref, mix_ref, h1_ref):
    step = pl.program_id(0)

    @pl.when(step == 0)
    def _():
        h1_ref[...] = jnp.zeros_like(h1_ref)

    @pl.when(step % blocks_per_seq == 0)
    def _():
        state_ref[...] = jnp.zeros_like(state_ref)

    mlp = _mlp_steps(h1_ref[(step + 1) % 2], gmlp_ref, wup_ref, wdown_ref, gmlp_post_ref, o_ref,
                     ODD_FF_CHUNK)
    h1_ref[step % 2] = _odd_mixer(
        mlp, h_ref[...], gpre_ref, win_ref, wlow_ref, wgu_ref, gbias_ref, tri_ref, ng_ref, wout_ref,
        gpost_ref, state_ref, mix_ref)


def _odd_layer(h, g_pre, w_in, w_low, w_gu, g_bias, tri, ng, w_out, g_post,
               g_mlp, w_up, w_down, g_mlp_post, seq):
    nt = seq // ROWS
    n_blocks = h.shape[0] // ROWS
    in_spec, out_spec = _row_specs(n_blocks)
    return pl.pallas_call(
        functools.partial(_odd_kernel, nt),
        out_shape=jax.ShapeDtypeStruct(h.shape, h.dtype),
        grid=(n_blocks + 1,),
        in_specs=[
            in_spec,
            _resident((1, D_MODEL)),
            _resident((D_MODEL, OD_END)),
            _resident((D_MODEL, RANK_PAD)),
            _resident((RANK_PAD, GLA_HEADS * GLA_DK)),
            _resident((1, GLA_HEADS * GLA_DK)),
            _resident((ROWS, ROWS)),
            _resident((1, D_MODEL)),
            _resident((D_MODEL, D_MODEL)),
            _resident((1, D_MODEL)),
            _resident((1, D_MODEL)),
            _resident((D_MODEL, D_FF)),
            _resident((D_FF, D_MODEL)),
            _resident((1, D_MODEL)),
        ],
        out_specs=out_spec,
        scratch_shapes=[
            pltpu.VMEM((GLA_HEADS, GLA_DK, GLA_DV), F32),
            pltpu.VMEM((ROWS, D_MODEL), BF16),
            pltpu.VMEM((2, ROWS, D_MODEL), F32),
        ],
        compiler_params=pltpu.CompilerParams(
            dimension_semantics=("arbitrary",), vmem_limit_bytes=VMEM_LIMIT),
        name="odd_layer",
    )(h, g_pre, w_in, w_low, w_gu, g_bias, tri, ng, w_out, g_post, g_mlp, w_up, w_down, g_mlp_post)


def _retention_tables(seq):
    heads = jnp.arange(RET_HEADS, dtype=F32)
    log_gamma = jnp.log1p(-(2.0 ** (-5.0 - heads)))
    idx = jnp.arange(ROWS, dtype=F32)
    rel = idx[:, None] - idx[None, :]
    causal = rel >= 0
    dec = jnp.where(causal[None], jnp.exp(log_gamma[:, None, None] * jnp.where(causal, rel, 0.0)[None]), 0.0)
    k_dec = jnp.exp(log_gamma[None, :] * (ROWS - 1 - idx)[:, None])
    q_dec = jnp.exp(log_gamma[None, :] * (idx + 1.0)[:, None])
    c_dec = jnp.exp(log_gamma * ROWS)
    shape = (RET_HEADS, ROWS, RET_D)
    kdec = jnp.broadcast_to(k_dec.T[:, :, None], shape)
    qdec = jnp.broadcast_to(q_dec.T[:, :, None], shape)
    cdec = jnp.broadcast_to(c_dec[:, None, None], (RET_HEADS, RET_D, RET_D))
    half = RET_D // 2
    inv = ROPE_BASE ** (-jnp.arange(half, dtype=F32) / half)
    ang = jnp.arange(seq, dtype=jnp.int32).astype(F32)[:, None] * inv[None, :]
    cos, sin = jnp.cos(ang), jnp.sin(ang)
    cos2 = jnp.concatenate([cos, cos], axis=-1)
    sin2 = jnp.concatenate([-sin, sin], axis=-1)
    return dec, qdec, kdec, cdec, cos2, sin2


def kernel(x, norm_g, w_up, w_down, ev_w_in, ev_ret_norm_g, ev_sinks, ev_w_out, od_w_in,
           od_w_gate_up, od_gate_bias, od_norm_g, od_w_out):
    batch, seq, d = x.shape
    depth = norm_g.shape[0]
    h = x.reshape(batch * seq, d)
    dec, qdec, kdec, cdec, cos2, sin2 = _retention_tables(seq)
    pos = np.arange(ROWS)
    tri = jnp.asarray((pos[None, :] <= pos[:, None])
                      & (pos[None, :] // GLA_CHUNK == pos[:, None] // GLA_CHUNK), BF16)
    w_up_b, w_down_b = w_up.astype(BF16), w_down.astype(BF16)
    n_ev = ev_w_in.shape[0]
    aq_cols = ev_w_in[:, :, EV_AQ:EV_AK].reshape(n_ev, d, SWA_KV_HEADS, SWA_GROUP, SWA_HD)
    aq_cols = aq_cols.transpose(0, 1, 3, 2, 4).reshape(n_ev, d, EV_AK - EV_AQ)
    ev_w_in_b = jnp.concatenate(
        [ev_w_in[:, :, :EV_AQ], aq_cols, ev_w_in[:, :, EV_AK:]], axis=-1).astype(BF16)
    att_rows = ev_w_out[:, EV_RET_MIX:, :].reshape(n_ev, SWA_KV_HEADS, SWA_GROUP, SWA_HD, d)
    att_rows = att_rows.transpose(0, 2, 1, 3, 4).reshape(n_ev, SWA_HEADS * SWA_HD, d)
    ev_w_out_b = jnp.concatenate([ev_w_out[:, :EV_RET_MIX, :], att_rows], axis=1).astype(BF16)
    od_w_in_b = od_w_in[:, :, :OD_END].astype(BF16)
    od_w_low_b = jnp.pad(od_w_in[:, :, OD_END:], ((0, 0), (0, 0), (0, RANK_PAD - GLA_RANK))).astype(BF16)
    od_w_gu_b = jnp.pad(od_w_gate_up, ((0, 0), (0, RANK_PAD - GLA_RANK), (0, 0))).astype(BF16)
    od_w_out_b = od_w_out.astype(BF16)
    g = lambda layer, j: norm_g[layer, j].reshape(1, d)
    for layer in range(depth):
        i = layer // 2
        mlp_args = (g(layer, 2), w_up_b[layer], w_down_b[layer], g(layer, 3))
        if layer % 2 == 0:
            h = _even_layer(h, ev_sinks[i], g(layer, 0), ev_w_in_b[i], cos2, sin2, dec, qdec, kdec, cdec,
                            ev_ret_norm_g[i].reshape(1, -1), ev_w_out_b[i], g(layer, 1), *mlp_args, seq)
        else:
            h = _odd_layer(h, g(layer, 0), od_w_in_b[i], od_w_low_b[i], od_w_gu_b[i],
                           od_gate_bias[i].reshape(1, -1), tri, od_norm_g[i].reshape(1, -1),
                           od_w_out_b[i], g(layer, 1), *mlp_args, seq)
    return h.reshape(batch, seq, d)
```

```python
import functools

import numpy as np
import jax
import jax.numpy as jnp
from jax import lax
from jax.experimental import pallas as pl
from jax.experimental.pallas import tpu as pltpu

F32 = jnp.float32
BF16 = jnp.bfloat16

D_MODEL = 1024
D_FF = 4 * D_MODEL
NORM_EPS = 1e-6
NEG_INF = -1e30

RET_HEADS = 4
RET_D = 128
ROPE_BASE = 10000.0

SWA_HEADS = 8
SWA_KV_HEADS = 2
SWA_GROUP = SWA_HEADS // SWA_KV_HEADS
SWA_HD = 64
WINDOW = 128

GLA_HEADS = 4
GLA_DK = 128
GLA_DV = 256
GLA_RANK = 16
GLA_TAU = 16.0
GLA_CHUNK = 64

LANES = 128
SUBLANES = 8
RANK_PAD = LANES

EV_RQ, EV_RK, EV_RV, EV_RG, EV_AQ, EV_AK, EV_AV, EV_END = 0, 512, 1024, 1536, 2048, 2560, 2688, 2816
EV_RET_MIX = RET_HEADS * RET_D
OD_Q, OD_K, OD_V, OD_R, OD_END = 0, 512, 1024, 2048, 3072

ROWS = 256
SUB_BLOCKS = 2
STEP_ROWS = SUB_BLOCKS * ROWS
EVEN_FF_CHUNK = 2048
ODD_FF_CHUNK = 2048
SWA_HEADS_PER_MLP_STEP = 4
EVEN_OUT_PIECES = 1
ODD_OUT_PIECES = 2
VMEM_LIMIT = 56 * 1024 * 1024

SWA_STACK_HEADS = tuple(h for p in range(SWA_GROUP) for h in (p, SWA_GROUP + p))


def _rms(x, g):
    return x * lax.rsqrt(jnp.mean(x * x, axis=-1, keepdims=True) + NORM_EPS) * g


def _silu(x):
    return x * (1.0 / (1.0 + jnp.exp(-x)))


def _dot(a, b):
    return jnp.dot(a.astype(BF16), b.astype(BF16), preferred_element_type=F32)


def _dot_nt(a, b):
    return lax.dot_general(a.astype(BF16), b.astype(BF16), (((1,), (1,)), ((), ())),
                           preferred_element_type=F32)


def _dot_tn(a, b):
    return lax.dot_general(a.astype(BF16), b.astype(BF16), (((0,), (0,)), ((), ())),
                           preferred_element_type=F32)


def _mlp_steps(x, gpre_ref, wup_ref, wdown_ref, gpost_ref, o_ref, rows, ff_chunk):
    u = _rms(x, gpre_ref[...]).astype(BF16)
    yield
    acc = jnp.zeros(x.shape, F32)
    n_chunks = D_FF // ff_chunk
    for c in range(n_chunks):
        cols = slice(c * ff_chunk, (c + 1) * ff_chunk)
        hid = jnp.maximum(jnp.dot(u, wup_ref[:, cols], preferred_element_type=F32), 0.0)
        hid = (hid * hid).astype(BF16)
        yield
        acc = acc + jnp.dot(hid, wdown_ref[cols, :], preferred_element_type=F32)
        if c == n_chunks - 1:
            o_ref[rows, :] = x + _rms(acc, gpost_ref[...])
        yield


def _out_proj(x, mix_ref, wout_ref, gpost_ref, n_pieces):
    pieces = []
    for p in range(n_pieces):
        rows = slice(p * ROWS // n_pieces, (p + 1) * ROWS // n_pieces)
        out = jnp.dot(mix_ref[rows, :], wout_ref[...], preferred_element_type=F32)
        pieces.append(x[rows] + _rms(out, gpost_ref[...]))
    return jnp.concatenate(pieces, axis=0)


def _even_mixer(mlp, first_in_seq, x, sinks_ref, gpre_ref, win_ref, cos, sin, dec_ref, qdec_ref, kdec_ref,
                cdec_ref, rng_ref, wout_ref, gpost_ref, state_ref, kprev_ref, vprev_ref, mix_ref):
    u = _rms(x, gpre_ref[...]).astype(BF16)
    next(mlp)

    def proj(a, b):
        return jnp.dot(u, win_ref[:, a:b], preferred_element_type=F32)

    rq = proj(EV_RQ, EV_RK)
    rk = proj(EV_RK, EV_RV)
    rv = proj(EV_RV, EV_RG)
    rg = proj(EV_RG, EV_AQ)
    heads = range(RET_HEADS)
    head_cols = [slice(hd * RET_D, (hd + 1) * RET_D) for hd in heads]
    qs = [rq[:, c] * cos + pltpu.roll(rq[:, c], RET_D // 2, 1) * sin for c in head_cols]
    ks = [(rk[:, c] * cos + pltpu.roll(rk[:, c], RET_D // 2, 1) * sin) * (RET_D ** -0.5)
          for c in head_cols]
    vs = [rv[:, c].astype(BF16) for c in head_cols]
    states = [state_ref[hd] for hd in heads]
    scores = [_dot_nt(qs[hd], ks[hd]) * dec_ref[hd] for hd in heads]
    cross = [_dot(qs[hd] * qdec_ref[hd], states[hd]) for hd in heads]
    for hd in heads:
        state_ref[hd] = cdec_ref[hd] * states[hd] + _dot_tn(ks[hd] * kdec_ref[hd], vs[hd])
    outs = [_dot(scores[hd], vs[hd]) + cross[hd] for hd in heads]
    for hd in heads:
        o = outs[hd]
        y = o * lax.rsqrt(jnp.mean(o * o, axis=-1, keepdims=True) + NORM_EPS)
        mix_ref[:, head_cols[hd]] = (y * rng_ref[:, head_cols[hd]] * _silu(rg[:, head_cols[hd]])).astype(BF16)

    aq = proj(EV_AQ, EV_AK) * (SWA_HD ** -0.5)
    ak = proj(EV_AK, EV_AV)
    av = proj(EV_AV, EV_END)
    low = lax.broadcasted_iota(jnp.int32, (WINDOW, LANES), 1) < SWA_HD
    qi =lax.broadcasted_iota(jnp.int32, (WINDOW, 2 * WINDOW), 0)
    ki = lax.broadcasted_iota(jnp.int32, (WINDOW, 2 * WINDOW), 1)
    allowed = (ki > qi) & (ki <= qi + WINDOW)
    first_mask = allowed & (ki >= jnp.where(first_in_seq, WINDOW, 0))
    for c in range(ROWS // WINDOW):
        rows = slice(c * WINDOW, (c + 1) * WINDOW)
        if c == 0:
            kp, vp = kprev_ref[...], vprev_ref[...]
            mask = first_mask
        else:
            prev = slice((c - 1) * WINDOW, c * WINDOW)
            kp, vp = ak[prev], av[prev]
            mask = allowed
        kk = jnp.concatenate([kp, ak[rows]], axis=0)
        vv = jnp.concatenate([vp, av[rows]], axis=0)
        q_parts = []
        for p in range(SWA_GROUP):
            qg = aq[rows, p * LANES:(p + 1) * LANES]
            q_parts.append(jnp.where(low, qg, 0.0))
            q_parts.append(jnp.where(low, 0.0, qg))
        s_all = _dot_nt(jnp.concatenate(q_parts, axis=0), kk)
        ps = []
        for j, head in enumerate(SWA_STACK_HEADS):
            s = jnp.where(mask, s_all[j * WINDOW:(j + 1) * WINDOW], NEG_INF)
            sink = sinks_ref[head]
            m = jnp.maximum(jnp.max(s, axis=-1, keepdims=True), sink)
            p_un = jnp.exp(s - m)
            denom = jnp.sum(p_un, axis=-1, keepdims=True) + jnp.exp(sink - m)
            ps.append((p_un * (1.0 / denom)).astype(BF16))
            if j % SWA_HEADS_PER_MLP_STEP == SWA_HEADS_PER_MLP_STEP - 1:
                next(mlp, None)
        o_all = _dot(jnp.concatenate(ps, axis=0), vv)
        for p in range(SWA_GROUP):
            o_lo = o_all[(2 * p) * WINDOW:(2 * p + 1) * WINDOW]
            o_hi = o_all[(2 * p + 1) * WINDOW:(2 * p + 2) * WINDOW]
            mix_ref[rows, EV_RET_MIX + p * LANES:EV_RET_MIX + (p + 1) * LANES] = (
                jnp.where(low, o_lo, o_hi).astype(BF16))
    kprev_ref[...] = ak[ROWS - WINDOW:]
    vprev_ref[...] = av[ROWS - WINDOW:]
    for _ in mlp:
        pass

    return _out_proj(x, mix_ref, wout_ref, gpost_ref, EVEN_OUT_PIECES)


def _even_kernel(steps_per_seq, sinks_ref, h_ref, gpre_ref, win_ref, cos_ref, sin_ref, dec_ref,
                 qdec_ref, kdec_ref, cdec_ref, rng_ref, wout_ref, gpost_ref,
                 gmlp_ref, wup_ref, wdown_ref, gmlp_post_ref, o_ref,
                 state_ref, kprev_ref, vprev_ref, mix_ref, h1_ref):
    step = pl.program_id(0)
    first_step_of_seq = step % steps_per_seq == 0

    @pl.when(step == 0)
    def _():
        h1_ref[...] = jnp.zeros_like(h1_ref)

    @pl.when(first_step_of_seq)
    def _():
        state_ref[...] = jnp.zeros_like(state_ref)
        kprev_ref[...] = jnp.zeros_like(kprev_ref)
        vprev_ref[...] = jnp.zeros_like(vprev_ref)

    for sb in range(SUB_BLOCKS):
        rows = slice(sb * ROWS, (sb + 1) * ROWS)
        mlp = _mlp_steps(h1_ref[(step + 1) % 2, rows], gmlp_ref, wup_ref, wdown_ref, gmlp_post_ref,
                         o_ref, rows, EVEN_FF_CHUNK)
        h1_ref[step % 2, rows] = _even_mixer(
            mlp, first_step_of_seq if sb == 0 else False, h_ref[rows, :], sinks_ref, gpre_ref, win_ref,
            cos_ref[rows, :], sin_ref[rows, :], dec_ref, qdec_ref, kdec_ref, cdec_ref, rng_ref, wout_ref,
            gpost_ref, state_ref, kprev_ref, vprev_ref, mix_ref.at[sb])


def _resident(shape):
    zeros = (0,) * len(shape)
    return pl.BlockSpec(shape, lambda s: zeros, pipeline_mode=pl.Buffered(1))


def _stacked(shape, index):
    zeros = (0,) * len(shape)
    return pl.BlockSpec((None,) + tuple(shape), lambda s: (index,) + zeros, pipeline_mode=pl.Buffered(1))


def _row_specs(n_blocks):
    in_spec = pl.BlockSpec((STEP_ROWS, D_MODEL), lambda s: (jnp.minimum(s, n_blocks - 1), 0))
    out_spec = pl.BlockSpec((STEP_ROWS, D_MODEL), lambda s: (jnp.maximum(s - 1, 0), 0))
    return in_spec, out_spec


def _even_layer(h, sinks, g_pre, w_in, cos2, sin2, dec, qdec, kdec, cdec, ret_g, w_out, g_post,
                g_mlp, w_up, w_down, g_mlp_post, seq, layer):
    nt = seq // STEP_ROWS
    n_blocks = h.shape[0] // STEP_ROWS
    in_spec, out_spec = _row_specs(n_blocks)
    pos_spec = pl.BlockSpec((STEP_ROWS, LANES), lambda s: (jnp.minimum(s, n_blocks - 1) % nt, 0))
    return pl.pallas_call(
        functools.partial(_even_kernel, nt),
        out_shape=jax.ShapeDtypeStruct(h.shape, h.dtype),
        grid=(n_blocks + 1,),
        in_specs=[
            pl.BlockSpec(memory_space=pltpu.SMEM),
            in_spec,
            _resident((1, D_MODEL)),
            _stacked((D_MODEL, EV_END), layer // 2),
            pos_spec, pos_spec,
            _resident((RET_HEADS, ROWS, ROWS)),
            _resident((RET_HEADS, ROWS, RET_D)),
            _resident((RET_HEADS, ROWS, RET_D)),
            _resident((RET_HEADS, RET_D, RET_D)),
            _resident((1, RET_HEADS * RET_D)),
            _stacked((D_MODEL, D_MODEL), layer // 2),
            _resident((1, D_MODEL)),
            _resident((1, D_MODEL)),
            _stacked((D_MODEL, D_FF), layer),
            _stacked((D_FF, D_MODEL), layer),
            _resident((1, D_MODEL)),
        ],
        out_specs=out_spec,
        scratch_shapes=[
            pltpu.VMEM((RET_HEADS, RET_D, RET_D), F32),
            pltpu.VMEM((WINDOW, LANES), F32),
            pltpu.VMEM((WINDOW, LANES), F32),
            pltpu.VMEM((SUB_BLOCKS, ROWS, D_MODEL), BF16),
            pltpu.VMEM((2, STEP_ROWS, D_MODEL), F32),
        ],
        compiler_params=pltpu.CompilerParams(
            dimension_semantics=("arbitrary",), vmem_limit_bytes=VMEM_LIMIT),
        name="even_layer",
    )(sinks, h, g_pre, w_in, cos2, sin2, dec, qdec, kdec, cdec, ret_g, w_out, g_post,
      g_mlp, w_up, w_down, g_mlp_post)


def _odd_mixer(mlp, x, gpre_ref, win_ref, wlow_ref, wgu_ref, gbias_ref, tri_ref, ng_ref, wout_ref,
               gpost_ref, state_ref, mix_ref):
    u = _rms(x, gpre_ref[...]).astype(BF16)
    next(mlp)

    def proj(a, b):
        return jnp.dot(u, win_ref[:, a:b], preferred_element_type=F32)

    a_low = jnp.dot(u, wlow_ref[...], preferred_element_type=F32)
    k = proj(OD_K, OD_V)
    gl = _dot(a_low, wgu_ref[...]) + gbias_ref[...]
    q = proj(OD_Q, OD_K) * (GLA_DK ** -0.5)
    log_g = (jnp.minimum(gl, 0.0) - jnp.log1p(jnp.exp(-jnp.abs(gl)))) * (1.0 / GLA_TAU)

    n_chunks = ROWS // GLA_CHUNK
    chunk_rows = [slice(c * GLA_CHUNK, (c + 1) * GLA_CHUNK) for c in range(n_chunks)]
    lg_hi = log_g.astype(BF16)
    lg_lo = (log_g - lg_hi.astype(F32)).astype(BF16)
    v = proj(OD_V, OD_R).astype(BF16)
    tri = tri_ref[...]
    b = (jnp.dot(tri, lg_hi, preferred_element_type=F32)
         + jnp.dot(tri, lg_lo, preferred_element_type=F32))
    r = proj(OD_R, OD_END)
    next(mlp, None)
    b_last =[b[rows.stop - 1:rows.stop, :] for rows in chunk_rows]
    b_last_full = jnp.concatenate(
        [jnp.broadcast_to(bl, (GLA_CHUNK, bl.shape[1])) for bl in b_last], axis=0)
    q_in = q * jnp.exp(b)
    k_in = k * jnp.exp(-b)
    k_st = k * jnp.exp(b_last_full - b)

    ri = lax.broadcasted_iota(jnp.int32, (ROWS, ROWS), 0)
    ci = lax.broadcasted_iota(jnp.int32, (ROWS, ROWS), 1)
    causal = (ci <= ri) & (ci >= ri - (ri & (GLA_CHUNK - 1)))
    row_id = lax.broadcasted_iota(jnp.int32, (SUBLANES, GLA_DK), 0)
    heads = range(GLA_HEADS)
    kcs = [slice(hd * GLA_DK, (hd + 1) * GLA_DK) for hd in heads]
    vcs = [slice(hd * GLA_DV, (hd + 1) * GLA_DV) for hd in heads]
    scores = [jnp.where(causal, _dot_nt(q_in[:, kcs[hd]], k_in[:, kcs[hd]]), 0.0) for hd in heads]
    kvs = [[_dot_tn(k_st[rows, kcs[hd]], v[rows, vcs[hd]]) for rows in chunk_rows] for hd in heads]
    crosses = []
    for hd in heads:
        tile = jnp.zeros((SUBLANES, GLA_DK), F32)
        for c in range(n_chunks):
            tile = jnp.where(row_id == c, b_last[c][:, kcs[hd]], tile)
        tile = jnp.concatenate([tile, jnp.zeros((GLA_DK - SUBLANES, GLA_DK), F32)], axis=0).T
        state = state_ref[hd]
        cross = []
        for c in range(n_chunks):
            cross.append(_dot(q_in[chunk_rows[c], kcs[hd]], state))
            state = jnp.exp(tile[:, c:c + 1]) * state + kvs[hd][c]
        state_ref[hd] = state
        crosses.append(jnp.concatenate(cross, axis=0))
    outs = [_dot(scores[hd], v[:, vcs[hd]]) + crosses[hd] for hd in heads]
    next(mlp, None)
    for hd in heads:
        o = outs[hd]
        y = o * lax.rsqrt(jnp.mean(o * o, axis=-1, keepdims=True) + NORM_EPS)
        mix_ref[:, vcs[hd]] = (y * ng_ref[:, vcs[hd]] * _silu(r[:, vcs[hd]])).astype(BF16)
    for _ in mlp:
        pass

    return _out_proj(x, mix_ref, wout_ref, gpost_ref, ODD_OUT_PIECES)


def _odd_kernel(steps_per_seq, h_ref, gpre_ref, win_ref, wlow_ref, wgu_ref, gbias_ref, tri_ref,
                ng_ref, wout_ref, gpost_ref, gmlp_ref, wup_ref, wdown_ref, gmlp_post_ref, o_ref,
                state_ref, mix_ref, h1_ref):
    step = pl.program_id(0)

    @pl.when(step == 0)
    def _():
        h1_ref[...] = jnp.zeros_like(h1_ref)

    @pl.when(step % steps_per_seq == 0)
    def _():
        state_ref[...] = jnp.zeros_like(state_ref)

    for sb in range(SUB_BLOCKS):
        rows = slice(sb * ROWS, (sb + 1) * ROWS)
        mlp = _mlp_steps(h1_ref[(step + 1) % 2, rows], gmlp_ref, wup_ref, wdown_ref, gmlp_post_ref,
                         o_ref, rows, ODD_FF_CHUNK)
        h1_ref[step % 2, rows] = _odd_mixer(
            mlp, h_ref[rows, :], gpre_ref, win_ref, wlow_ref, wgu_ref, gbias_ref, tri_ref, ng_ref,
            wout_ref, gpost_ref, state_ref, mix_ref.at[sb])


def _odd_layer(h, g_pre, w_in, w_low, w_gu, g_bias, tri, ng, w_out, g_post,
               g_mlp, w_up, w_down, g_mlp_post, seq, layer):
    nt = seq // STEP_ROWS
    n_blocks = h.shape[0] // STEP_ROWS
    in_spec, out_spec = _row_specs(n_blocks)
    return pl.pallas_call(
        functools.partial(_odd_kernel, nt),
        out_shape=jax.ShapeDtypeStruct(h.shape, h.dtype),
        grid=(n_blocks + 1,),
        in_specs=[
            in_spec,
            _resident((1, D_MODEL)),
            _stacked((D_MODEL, OD_END), layer // 2),
            _stacked((D_MODEL, RANK_PAD), layer // 2),
            _stacked((RANK_PAD, GLA_HEADS * GLA_DK), layer // 2),
            _resident((1, GLA_HEADS * GLA_DK)),
            _resident((ROWS, ROWS)),
            _resident((1, D_MODEL)),
            _stacked((D_MODEL, D_MODEL), layer // 2),
            _resident((1, D_MODEL)),
            _resident((1, D_MODEL)),
            _stacked((D_MODEL, D_FF), layer),
            _stacked((D_FF, D_MODEL), layer),
            _resident((1, D_MODEL)),
        ],
        out_specs=out_spec,
        scratch_shapes=[
            pltpu.VMEM((GLA_HEADS, GLA_DK, GLA_DV), F32),
            pltpu.VMEM((SUB_BLOCKS, ROWS, D_MODEL), BF16),
            pltpu.VMEM((2, STEP_ROWS, D_MODEL), F32),
        ],
        compiler_params=pltpu.CompilerParams(
            dimension_semantics=("arbitrary",), vmem_limit_bytes=VMEM_LIMIT),
        name="odd_layer",
    )(h, g_pre, w_in, w_low, w_gu, g_bias, tri, ng, w_out, g_post, g_mlp, w_up, w_down, g_mlp_post)


def _retention_tables(seq):
    heads = jnp.arange(RET_HEADS, dtype=F32)
    log_gamma = jnp.log1p(-(2.0 ** (-5.0 - heads)))
    idx = jnp.arange(ROWS, dtype=F32)
    rel = idx[:, None] - idx[None, :]
    causal = rel >= 0
    dec = jnp.where(causal[None], jnp.exp(log_gamma[:, None, None] * jnp.where(causal, rel, 0.0)[None]), 0.0)
    k_dec = jnp.exp(log_gamma[None, :] * (ROWS - 1 - idx)[:, None])
    q_dec = jnp.exp(log_gamma[None, :] * (idx + 1.0)[:, None])
    c_dec = jnp.exp(log_gamma * ROWS)
    shape = (RET_HEADS, ROWS, RET_D)
    kdec = jnp.broadcast_to(k_dec.T[:, :, None], shape)
    qdec = jnp.broadcast_to(q_dec.T[:, :, None], shape)
    cdec = jnp.broadcast_to(c_dec[:, None, None], (RET_HEADS, RET_D, RET_D))
    half = RET_D // 2
    inv = ROPE_BASE ** (-jnp.arange(half, dtype=F32) / half)
    ang = jnp.arange(seq, dtype=jnp.int32).astype(F32)[:, None] * inv[None, :]
    cos, sin = jnp.cos(ang), jnp.sin(ang)
    cos2 = jnp.concatenate([cos, cos], axis=-1)
    sin2 = jnp.concatenate([-sin, sin], axis=-1)
    return dec, qdec, kdec, cdec, cos2, sin2


def kernel(x, norm_g, w_up, w_down, ev_w_in, ev_ret_norm_g, ev_sinks, ev_w_out, od_w_in,
           od_w_gate_up, od_gate_bias, od_norm_g, od_w_out):
    batch, seq, d = x.shape
    depth = norm_g.shape[0]
    h = x.reshape(batch * seq, d)
    dec, qdec, kdec, cdec, cos2, sin2 = _retention_tables(seq)
    pos = np.arange(ROWS)
    tri = jnp.asarray((pos[None, :] <= pos[:, None])
                      & (pos[None, :] // GLA_CHUNK == pos[:, None] // GLA_CHUNK), BF16)
    w_up_b, w_down_b = w_up.astype(BF16), w_down.astype(BF16)
    n_ev = ev_w_in.shape[0]
    aq_cols = ev_w_in[:, :, EV_AQ:EV_AK].reshape(n_ev, d, SWA_KV_HEADS, SWA_GROUP, SWA_HD)
    aq_cols = aq_cols.transpose(0, 1, 3, 2, 4).reshape(n_ev, d, EV_AK - EV_AQ)
    ev_w_in_b = jnp.concatenate(
        [ev_w_in[:, :, :EV_AQ], aq_cols, ev_w_in[:, :, EV_AK:]], axis=-1).astype(BF16)
    att_rows = ev_w_out[:, EV_RET_MIX:, :].reshape(n_ev, SWA_KV_HEADS, SWA_GROUP, SWA_HD, d)
    att_rows = att_rows.transpose(0, 2, 1, 3, 4).reshape(n_ev, SWA_HEADS * SWA_HD, d)
    ev_w_out_b = jnp.concatenate([ev_w_out[:, :EV_RET_MIX, :], att_rows], axis=1).astype(BF16)
    od_w_in_b = od_w_in.astype(BF16)
    od_w_low_b = jnp.pad(od_w_in[:, :, OD_END:], ((0, 0), (0, 0), (0, RANK_PAD - GLA_RANK))).astype(BF16)
    od_w_gu_b = jnp.pad(od_w_gate_up, ((0, 0), (0, RANK_PAD - GLA_RANK), (0, 0))).astype(BF16)
    od_w_out_b = od_w_out.astype(BF16)
    g = lambda layer, j: norm_g[layer, j].reshape(1, d)
    for layer in range(depth):
        i = layer // 2
        mlp_args = (g(layer, 2), w_up_b, w_down_b, g(layer, 3), seq, layer)
        if layer % 2 == 0:
            h = _even_layer(h, ev_sinks[i], g(layer, 0), ev_w_in_b, cos2, sin2, dec, qdec, kdec, cdec,
                            ev_ret_norm_g[i].reshape(1, -1), ev_w_out_b, g(layer, 1), *mlp_args)
        else:
            h = _odd_layer(h, g(layer, 0), od_w_in_b, od_w_low_b, od_w_gu_b,
                           od_gate_bias[i].reshape(1, -1), tri, od_norm_g[i].reshape(1, -1),
                           od_w_out_b, g(layer, 1), *mlp_args)
    return h.reshape(batch, seq, d)
```

```python
import functools

import numpy as np
import jax
import jax.numpy as jnp
from jax import lax
from jax.experimental import pallas as pl
from jax.experimental.pallas import tpu as pltpu

F32 = jnp.float32
BF16 = jnp.bfloat16

D_MODEL = 1024
D_FF = 4 * D_MODEL
NORM_EPS = 1e-6
NEG_INF = -1e30

RET_HEADS = 4
RET_D = 128
ROPE_BASE = 10000.0

SWA_HEADS = 8
SWA_KV_HEADS = 2
SWA_GROUP = SWA_HEADS // SWA_KV_HEADS
SWA_HD = 64
WINDOW = 128

GLA_HEADS = 4
GLA_DK = 128
GLA_DV = 256
GLA_RANK = 16
GLA_TAU = 16.0
GLA_CHUNK = 64

LANES = 128
SUBLANES = 8
RANK_PAD = LANES

EV_RQ, EV_RK, EV_RV, EV_RG, EV_AQ, EV_AK, EV_AV, EV_END = 0, 512, 1024, 1536, 2048, 2560, 2688, 2816
EV_RET_MIX = RET_HEADS * RET_D
OD_Q, OD_K, OD_V, OD_R, OD_END = 0, 512, 1024, 2048, 3072

ROWS = 256
SUB_BLOCKS = 2
STEP_ROWS = SUB_BLOCKS * ROWS
EVEN_FF_CHUNK = 2048
ODD_FF_CHUNK = 2048
SWA_HEADS_PER_MLP_STEP = 4
EVEN_OUT_PIECES = 1
ODD_OUT_PIECES = 2
VMEM_LIMIT = 56 * 1024 * 1024

SWA_STACK_HEADS = tuple(h for p in range(SWA_GROUP) for h in (p, SWA_GROUP + p))


def _rms(x, g):
    return x * lax.rsqrt(jnp.mean(x * x, axis=-1, keepdims=True) + NORM_EPS) * g


def _silu(x):
    return x * (1.0 / (1.0 + jnp.exp(-x)))


def _dot(a, b):
    return jnp.dot(a.astype(BF16), b.astype(BF16), preferred_element_type=F32)


def _dot_nt(a, b):
    return lax.dot_general(a.astype(BF16), b.astype(BF16), (((1,), (1,)), ((), ())),
                           preferred_element_type=F32)


def _dot_tn(a, b):
    return lax.dot_general(a.astype(BF16), b.astype(BF16), (((0,), (0,)), ((), ())),
                           preferred_element_type=F32)


def _mlp_steps(x, gpre_ref, wup_ref, wdown_ref, gpost_ref, o_ref, rows, ff_chunk):
    u = _rms(x, gpre_ref[...]).astype(BF16)
    yield
    acc = jnp.zeros(x.shape, F32)
    n_chunks = D_FF // ff_chunk
    for c in range(n_chunks):
        cols = slice(c * ff_chunk, (c + 1) * ff_chunk)
        hid = jnp.maximum(jnp.dot(u, wup_ref[:, cols], preferred_element_type=F32), 0.0)
        hid = (hid * hid).astype(BF16)
        yield
        acc = acc + jnp.dot(hid, wdown_ref[cols, :], preferred_element_type=F32)
        if c == n_chunks - 1:
            o_ref[rows, :] = x + _rms(acc, gpost_ref[...])
        yield


def _run_pipeline_step(step, body):
    last = pl.num_programs(0) - 1
    pl.when(step == 0)(lambda: body(True, False))
    pl.when((step > 0) & (step < last))(lambda: body(True, True))
    pl.when(step == last)(lambda: body(False, True))


def _out_proj(x, mix_ref, wout_ref, gpost_ref, n_pieces):
    pieces = []
    for p in range(n_pieces):
        rows = slice(p * ROWS // n_pieces, (p + 1) * ROWS // n_pieces)
        out = jnp.dot(mix_ref[rows, :], wout_ref[...], preferred_element_type=F32)
        pieces.append(x[rows] + _rms(out, gpost_ref[...]))
    return jnp.concatenate(pieces, axis=0)


def _even_mixer(mlp, first_in_seq, x, sinks_ref, gpre_ref, win_ref, cos, sin, dec_ref, qdec_ref, kdec_ref,
                cdec_ref, rng_ref, wout_ref, gpost_ref, state_ref, kprev_ref, vprev_ref, mix_ref):
    u = _rms(x, gpre_ref[...]).astype(BF16)
    next(mlp, None)

    def proj(a, b):
        return jnp.dot(u, win_ref[:, a:b], preferred_element_type=F32)

    rq = proj(EV_RQ, EV_RK)
    rk = proj(EV_RK, EV_RV)
    rv = proj(EV_RV, EV_RG)
    rg = proj(EV_RG, EV_AQ)
    heads = range(RET_HEADS)
    head_cols = [slice(hd * RET_D, (hd + 1) * RET_D) for hd in heads]
    qs = [rq[:, c] * cos + pltpu.roll(rq[:, c], RET_D // 2, 1) * sin for c in head_cols]
    ks = [(rk[:, c] * cos + pltpu.roll(rk[:, c], RET_D // 2, 1) * sin) * (RET_D ** -0.5)
          for c in head_cols]
    vs = [rv[:, c].astype(BF16) for c in head_cols]
    states = [state_ref[hd] for hd in heads]
    scores = [_dot_nt(qs[hd], ks[hd]) * dec_ref[hd] for hd in heads]
    cross = [_dot(qs[hd] * qdec_ref[hd], states[hd]) for hd in heads]
    for hd in heads:
        state_ref[hd] = cdec_ref[hd] * states[hd] + _dot_tn(ks[hd] * kdec_ref[hd], vs[hd])
    outs = [_dot(scores[hd], vs[hd]) + cross[hd] for hd in heads]
    for hd in heads:
        o = outs[hd]
        y = o * lax.rsqrt(jnp.mean(o * o, axis=-1, keepdims=True) + NORM_EPS)
        mix_ref[:, head_cols[hd]] = (y * rng_ref[:, head_cols[hd]] * _silu(rg[:, head_cols[hd]])).astype(BF16)

    aq = proj(EV_AQ, EV_AK) * (SWA_HD ** -0.5)
    ak = proj(EV_AK, EV_AV)
    av = proj(EV_AV, EV_END)
    low = lax.broadcasted_iota(jnp.int32, (WINDOW, LANES), 1) < SWA_HD
    qi = lax.broadcasted_iota(jnp.int32, (WINDOW, 2 * WINDOW), 0)
    ki = lax.broadcasted_iota(jnp.int32, (WINDOW, 2 * WINDOW), 1)
    allowed = (ki > qi) & (ki <= qi + WINDOW)
    first_mask = allowed & (ki >= jnp.where(first_in_seq, WINDOW, 0))
    for c in range(ROWS // WINDOW):
        rows = slice(c * WINDOW, (c + 1) * WINDOW)
        if c == 0:
            kp, vp = kprev_ref[...], vprev_ref[...]
            mask = first_mask
        else:
            prev = slice((c - 1) * WINDOW, c * WINDOW)
            kp, vp = ak[prev], av[prev]
            mask = allowed
        kk = jnp.concatenate([kp, ak[rows]], axis=0)
        vv = jnp.concatenate([vp, av[rows]], axis=0)
        q_parts = []
        for p in range(SWA_GROUP):
            qg = aq[rows, p * LANES:(p + 1) * LANES]
            q_parts.append(jnp.where(low, qg, 0.0))
            q_parts.append(jnp.where(low, 0.0, qg))
        s_all = _dot_nt(jnp.concatenate(q_parts, axis=0), kk)
        ps = []
        for j, head in enumerate(SWA_STACK_HEADS):
            s = jnp.where(mask, s_all[j * WINDOW:(j + 1) * WINDOW], NEG_INF)
            sink = sinks_ref[head]
            m = jnp.maximum(jnp.max(s, axis=-1, keepdims=True), sink)
            p_un = jnp.exp(s - m)
            denom = jnp.sum(p_un, axis=-1, keepdims=True) + jnp.exp(sink - m)
            ps.append((p_un * (1.0 / denom)).astype(BF16))
            if j % SWA_HEADS_PER_MLP_STEP == SWA_HEADS_PER_MLP_STEP - 1:
                next(mlp, None)
        o_all = _dot(jnp.concatenate(ps, axis=0), vv)
        for p in range(SWA_GROUP):
            o_lo = o_all[(2 * p) * WINDOW:(2 * p + 1) * WINDOW]
            o_hi = o_all[(2 * p + 1) * WINDOW:(2 * p + 2) * WINDOW]
            mix_ref[rows, EV_RET_MIX + p * LANES:EV_RET_MIX + (p + 1) * LANES] = (
                jnp.where(low, o_lo, o_hi).astype(BF16))
    kprev_ref[...] = ak[ROWS - WINDOW:]
    vprev_ref[...] = av[ROWS - WINDOW:]
    for _ in mlp:
        pass

    return _out_proj(x, mix_ref, wout_ref, gpost_ref, EVEN_OUT_PIECES)


def _even_kernel(steps_per_seq, sinks_ref, h_ref, gpre_ref, win_ref, cos_ref, sin_ref, dec_ref,
                 qdec_ref, kdec_ref, cdec_ref, rng_ref, wout_ref, gpost_ref,
                 gmlp_ref, wup_ref, wdown_ref, gmlp_post_ref, o_ref,
                 state_ref, kprev_ref, vprev_ref, mix_ref, h1_ref):
    step = pl.program_id(0)
    first_step_of_seq = step % steps_per_seq == 0

    @pl.when(first_step_of_seq)
    def _():
        state_ref[...] = jnp.zeros_like(state_ref)
        kprev_ref[...] = jnp.zeros_like(kprev_ref)
        vprev_ref[...] = jnp.zeros_like(vprev_ref)

    def body(run_mixer, run_mlp):
        for sb in range(SUB_BLOCKS):
            rows = slice(sb * ROWS, (sb + 1) * ROWS)
            mlp = iter(())
            if run_mlp:
                mlp = _mlp_steps(h1_ref[(step + 1) % 2, rows], gmlp_ref, wup_ref, wdown_ref,
                                 gmlp_post_ref, o_ref, rows, EVEN_FF_CHUNK)
            if run_mixer:
                h1_ref[step % 2, rows] = _even_mixer(
                    mlp, first_step_of_seq if sb == 0 else False, h_ref[rows, :], sinks_ref, gpre_ref,
                    win_ref, cos_ref[rows, :], sin_ref[rows, :], dec_ref, qdec_ref, kdec_ref, cdec_ref,
                    rng_ref, wout_ref, gpost_ref, state_ref, kprev_ref, vprev_ref, mix_ref.at[sb])
            for _ in mlp:
                pass

    _run_pipeline_step(step, body)


def _resident(shape):
    zeros = (0,) * len(shape)
    return pl.BlockSpec(shape, lambda s: zeros, pipeline_mode=pl.Buffered(1))


def _stacked(shape, index):
    zeros = (0,) * len(shape)
    return pl.BlockSpec((None,) + tuple(shape), lambda s: (index,) + zeros, pipeline_mode=pl.Buffered(1))


def _row_specs(n_blocks):
    in_spec = pl.BlockSpec((STEP_ROWS, D_MODEL), lambda s: (jnp.minimum(s, n_blocks - 1), 0))
    out_spec = pl.BlockSpec((STEP_ROWS, D_MODEL), lambda s: (jnp.maximum(s - 1, 0), 0))
    return in_spec, out_spec


def _even_layer(h, sinks, g_pre, w_in, cos2, sin2, dec, qdec, kdec, cdec, ret_g, w_out, g_post,
                g_mlp, w_up, w_down, g_mlp_post, seq, layer):
    nt = seq // STEP_ROWS
    n_blocks = h.shape[0] // STEP_ROWS
    in_spec, out_spec = _row_specs(n_blocks)
    pos_spec = pl.BlockSpec((STEP_ROWS, LANES), lambda s: (jnp.minimum(s, n_blocks - 1) % nt, 0))
    return pl.pallas_call(
        functools.partial(_even_kernel, nt),
        out_shape=jax.ShapeDtypeStruct(h.shape, h.dtype),
        grid=(n_blocks + 1,),
        in_specs=[
            pl.BlockSpec(memory_space=pltpu.SMEM),
            in_spec,
            _resident((1, D_MODEL)),
            _stacked((D_MODEL, EV_END), layer // 2),
            pos_spec, pos_spec,
            _resident((RET_HEADS, ROWS, ROWS)),
            _resident((RET_HEADS, ROWS, RET_D)),
            _resident((RET_HEADS, ROWS, RET_D)),
            _resident((RET_HEADS, RET_D, RET_D)),
            _resident((1, RET_HEADS * RET_D)),
            _stacked((D_MODEL, D_MODEL), layer // 2),
            _resident((1, D_MODEL)),
            _resident((1, D_MODEL)),
            _stacked((D_MODEL, D_FF), layer),
            _stacked((D_FF, D_MODEL), layer),
            _resident((1, D_MODEL)),
        ],
        out_specs=out_spec,
        scratch_shapes=[
            pltpu.VMEM((RET_HEADS, RET_D, RET_D), F32),
            pltpu.VMEM((WINDOW, LANES), F32),
            pltpu.VMEM((WINDOW, LANES), F32),
            pltpu.VMEM((SUB_BLOCKS, ROWS, D_MODEL), BF16),
            pltpu.VMEM((2, STEP_ROWS, D_MODEL), F32),
        ],
        compiler_params=pltpu.CompilerParams(
            dimension_semantics=("arbitrary",), vmem_limit_bytes=VMEM_LIMIT),
        name="even_layer",
    )(sinks, h, g_pre, w_in, cos2, sin2, dec, qdec, kdec, cdec, ret_g, w_out, g_post,
      g_mlp, w_up, w_down, g_mlp_post)


def _odd_mixer(mlp, x, gpre_ref, win_ref, wlow_ref, wgu_ref, gbias_ref, tri_ref, ng_ref, wout_ref,
               gpost_ref, state_ref, mix_ref):
    u = _rms(x, gpre_ref[...]).astype(BF16)
    next(mlp, None)

    def proj(a, b):
        return jnp.dot(u, win_ref[:, a:b], preferred_element_type=F32)

    a_low = jnp.dot(u, wlow_ref[...], preferred_element_type=F32)
    k = proj(OD_K, OD_V)
    gl = _dot(a_low, wgu_ref[...]) + gbias_ref[...]
    q = proj(OD_Q, OD_K) * (GLA_DK ** -0.5)
    log_g = (jnp.minimum(gl, 0.0) - jnp.log1p(jnp.exp(-jnp.abs(gl)))) * (1.0 / GLA_TAU)

    n_chunks = ROWS // GLA_CHUNK
    chunk_rows = [slice(c * GLA_CHUNK, (c + 1) * GLA_CHUNK) for c in range(n_chunks)]
    lg_hi = log_g.astype(BF16)
    lg_lo = (log_g - lg_hi.astype(F32)).astype(BF16)
    v = proj(OD_V, OD_R).astype(BF16)
    tri = tri_ref[...]
    b = (jnp.dot(tri, lg_hi, preferred_element_type=F32)
         + jnp.dot(tri, lg_lo, preferred_element_type=F32))
    r = proj(OD_R, OD_END)
    next(mlp, None)
    b_last =[b[rows.stop - 1:rows.stop, :] for rows in chunk_rows]
    b_last_full = jnp.concatenate(
        [jnp.broadcast_to(bl, (GLA_CHUNK, bl.shape[1])) for bl in b_last], axis=0)
    q_in = q * jnp.exp(b)
    k_in = k * jnp.exp(-b)
    k_st = k * jnp.exp(b_last_full - b)

    ri = lax.broadcasted_iota(jnp.int32, (ROWS, ROWS), 0)
    ci = lax.broadcasted_iota(jnp.int32, (ROWS, ROWS), 1)
    causal = (ci <= ri) & (ci >= ri - (ri & (GLA_CHUNK - 1)))
    row_id = lax.broadcasted_iota(jnp.int32, (SUBLANES, GLA_DK), 0)
    heads = range(GLA_HEADS)
    kcs = [slice(hd * GLA_DK, (hd + 1) * GLA_DK) for hd in heads]
    vcs = [slice(hd * GLA_DV, (hd + 1) * GLA_DV) for hd in heads]
    scores = [jnp.where(causal, _dot_nt(q_in[:, kcs[hd]], k_in[:, kcs[hd]]), 0.0) for hd in heads]
    kvs = [[_dot_tn(k_st[rows, kcs[hd]], v[rows, vcs[hd]]) for rows in chunk_rows] for hd in heads]
    crosses = []
    for hd in heads:
        tile = jnp.zeros((SUBLANES, GLA_DK), F32)
        for c in range(n_chunks):
            tile = jnp.where(row_id == c, b_last[c][:, kcs[hd]], tile)
        tile = jnp.concatenate([tile, jnp.zeros((GLA_DK - SUBLANES, GLA_DK), F32)], axis=0).T
        state = state_ref[hd]
        cross = []
        for c in range(n_chunks):
            cross.append(_dot(q_in[chunk_rows[c], kcs[hd]], state))
            state = jnp.exp(tile[:, c:c + 1]) * state + kvs[hd][c]
        state_ref[hd] = state
        crosses.append(jnp.concatenate(cross, axis=0))
    outs = [_dot(scores[hd], v[:, vcs[hd]]) + crosses[hd] for hd in heads]
    next(mlp, None)
    for hd in heads:
        o = outs[hd]
        y = o * lax.rsqrt(jnp.mean(o * o, axis=-1, keepdims=True) + NORM_EPS)
        mix_ref[:, vcs[hd]] = (y * ng_ref[:, vcs[hd]] * _silu(r[:, vcs[hd]])).astype(BF16)
    for _ in mlp:
        pass

    return _out_proj(x, mix_ref, wout_ref, gpost_ref, ODD_OUT_PIECES)


def _odd_kernel(steps_per_seq, h_ref, gpre_ref, win_ref, wlow_ref, wgu_ref, gbias_ref, tri_ref,
                ng_ref, wout_ref, gpost_ref, gmlp_ref, wup_ref, wdown_ref, gmlp_post_ref, o_ref,
                state_ref, mix_ref, h1_ref):
    step = pl.program_id(0)

    @pl.when(step % steps_per_seq == 0)
    def _():
        state_ref[...] = jnp.zeros_like(state_ref)

    def body(run_mixer, run_mlp):
        for sb in range(SUB_BLOCKS):
            rows = slice(sb * ROWS, (sb + 1) * ROWS)
            mlp = iter(())
            if run_mlp:
                mlp = _mlp_steps(h1_ref[(step + 1) % 2, rows], gmlp_ref, wup_ref, wdown_ref,
                                 gmlp_post_ref, o_ref, rows, ODD_FF_CHUNK)
            if run_mixer:
                h1_ref[step % 2, rows] = _odd_mixer(
                    mlp, h_ref[rows, :], gpre_ref, win_ref, wlow_ref, wgu_ref, gbias_ref, tri_ref,
                    ng_ref, wout_ref, gpost_ref, state_ref, mix_ref.at[sb])
            for _ in mlp:
                pass

    _run_pipeline_step(step, body)


def _odd_layer(h, g_pre, w_in, w_low, w_gu, g_bias, tri, ng, w_out, g_post,
               g_mlp, w_up, w_down, g_mlp_post, seq, layer):
    nt = seq // STEP_ROWS
    n_blocks = h.shape[0] // STEP_ROWS
    in_spec, out_spec = _row_specs(n_blocks)
    return pl.pallas_call(
        functools.partial(_odd_kernel, nt),
        out_shape=jax.ShapeDtypeStruct(h.shape, h.dtype),
        grid=(n_blocks + 1,),
        in_specs=[
            in_spec,
            _resident((1, D_MODEL)),
            _stacked((D_MODEL, OD_END), layer // 2),
            _stacked((D_MODEL, RANK_PAD), layer // 2),
            _stacked((RANK_PAD, GLA_HEADS * GLA_DK), layer // 2),
            _resident((1, GLA_HEADS * GLA_DK)),
            _resident((ROWS, ROWS)),
            _resident((1, D_MODEL)),
            _stacked((D_MODEL, D_MODEL), layer // 2),
            _resident((1, D_MODEL)),
            _resident((1, D_MODEL)),
            _stacked((D_MODEL, D_FF), layer),
            _stacked((D_FF, D_MODEL), layer),
            _resident((1, D_MODEL)),
        ],
        out_specs=out_spec,
        scratch_shapes=[
            pltpu.VMEM((GLA_HEADS, GLA_DK, GLA_DV), F32),
            pltpu.VMEM((SUB_BLOCKS, ROWS, D_MODEL), BF16),
            pltpu.VMEM((2, STEP_ROWS, D_MODEL), F32),
        ],
        compiler_params=pltpu.CompilerParams(
            dimension_semantics=("arbitrary",), vmem_limit_bytes=VMEM_LIMIT),
        name="odd_layer",
    )(h, g_pre, w_in, w_low, w_gu, g_bias, tri, ng, w_out, g_post, g_mlp, w_up, w_down, g_mlp_post)


def _retention_tables(seq):
    heads = jnp.arange(RET_HEADS, dtype=F32)
    log_gamma = jnp.log1p(-(2.0 ** (-5.0 - heads)))
    idx = jnp.arange(ROWS, dtype=F32)
    rel = idx[:, None] - idx[None, :]
    causal = rel >= 0
    dec = jnp.where(causal[None], jnp.exp(log_gamma[:, None, None] * jnp.where(causal, rel, 0.0)[None]), 0.0)
    k_dec = jnp.exp(log_gamma[None, :] * (ROWS - 1 - idx)[:, None])
    q_dec = jnp.exp(log_gamma[None, :] * (idx + 1.0)[:, None])
    c_dec = jnp.exp(log_gamma * ROWS)
    shape = (RET_HEADS, ROWS, RET_D)
    kdec = jnp.broadcast_to(k_dec.T[:, :, None], shape)
    qdec = jnp.broadcast_to(q_dec.T[:, :, None], shape)
    cdec = jnp.broadcast_to(c_dec[:, None, None], (RET_HEADS, RET_D, RET_D))
    half = RET_D // 2
    inv = ROPE_BASE ** (-jnp.arange(half, dtype=F32) / half)
    ang = jnp.arange(seq, dtype=jnp.int32).astype(F32)[:, None] * inv[None, :]
    cos, sin = jnp.cos(ang), jnp.sin(ang)
    cos2 = jnp.concatenate([cos, cos], axis=-1)
    sin2 = jnp.concatenate([-sin, sin], axis=-1)
    return dec, qdec, kdec, cdec, cos2, sin2


def kernel(x, norm_g, w_up, w_down, ev_w_in, ev_ret_norm_g, ev_sinks, ev_w_out, od_w_in,
           od_w_gate_up, od_gate_bias, od_norm_g, od_w_out):
    batch, seq, d = x.shape
    depth = norm_g.shape[0]
    h = x.reshape(batch * seq, d)
    dec, qdec, kdec, cdec, cos2, sin2 = _retention_tables(seq)
    pos = np.arange(ROWS)
    tri = jnp.asarray((pos[None, :] <= pos[:, None])
                      & (pos[None, :] // GLA_CHUNK == pos[:, None] // GLA_CHUNK), BF16)
    w_up_b, w_down_b = w_up.astype(BF16), w_down.astype(BF16)
    n_ev = ev_w_in.shape[0]
    aq_cols = ev_w_in[:, :, EV_AQ:EV_AK].reshape(n_ev, d, SWA_KV_HEADS, SWA_GROUP, SWA_HD)
    aq_cols = aq_cols.transpose(0, 1, 3, 2, 4).reshape(n_ev, d, EV_AK - EV_AQ)
    ev_w_in_b = jnp.concatenate(
        [ev_w_in[:, :, :EV_AQ], aq_cols, ev_w_in[:, :, EV_AK:]], axis=-1).astype(BF16)
    att_rows = ev_w_out[:, EV_RET_MIX:, :].reshape(n_ev, SWA_KV_HEADS, SWA_GROUP, SWA_HD, d)
    att_rows = att_rows.transpose(0, 2, 1, 3, 4).reshape(n_ev, SWA_HEADS * SWA_HD, d)
    ev_w_out_b = jnp.concatenate([ev_w_out[:, :EV_RET_MIX, :], att_rows], axis=1).astype(BF16)
    od_w_in_b = od_w_in.astype(BF16)
    od_w_low_b = jnp.pad(od_w_in[:, :, OD_END:], ((0, 0), (0, 0), (0, RANK_PAD - GLA_RANK))).astype(BF16)
    od_w_gu_b = jnp.pad(od_w_gate_up, ((0, 0), (0, RANK_PAD - GLA_RANK), (0, 0))).astype(BF16)
    od_w_out_b = od_w_out.astype(BF16)
    g = lambda layer, j: norm_g[layer, j].reshape(1, d)
    for layer in range(depth):
        i = layer // 2
        mlp_args = (g(layer, 2), w_up_b, w_down_b, g(layer, 3), seq, layer)
        if layer % 2 == 0:
            h = _even_layer(h, ev_sinks[i], g(layer, 0), ev_w_in_b, cos2, sin2, dec, qdec, kdec, cdec,
                            ev_ret_norm_g[i].reshape(1, -1), ev_w_out_b, g(layer, 1), *mlp_args)
        else:
            h = _odd_layer(h, g(layer, 0), od_w_in_b, od_w_low_b, od_w_gu_b,
                           od_gate_bias[i].reshape(1, -1), tri, od_norm_g[i].reshape(1, -1),
                           od_w_out_b, g(layer, 1), *mlp_args)
    return h.reshape(batch, seq, d)
```

```python
import functools

import numpy as np
import jax
import jax.numpy as jnp
from jax import lax
from jax.experimental import pallas as pl
from jax.experimental.pallas import tpu as pltpu

F32 = jnp.float32
BF16 = jnp.bfloat16

D_MODEL = 1024
D_FF = 4 * D_MODEL
NORM_EPS = 1e-6
NEG_INF = -1e30

RET_HEADS = 4
RET_D = 128
ROPE_BASE = 10000.0

SWA_HEADS = 8
SWA_KV_HEADS = 2
SWA_GROUP = SWA_HEADS // SWA_KV_HEADS
SWA_HD = 64
WINDOW = 128

GLA_HEADS = 4
GLA_DK = 128
GLA_DV = 256
GLA_RANK = 16
GLA_TAU = 16.0
GLA_CHUNK = 64

LANES = 128
SUBLANES = 8
RANK_PAD = LANES

EV_RQ, EV_RK, EV_RV, EV_RG, EV_AQ, EV_AK, EV_AV, EV_END = 0, 512, 1024, 1536, 2048, 2560, 2688, 2816
EV_RET_MIX = RET_HEADS * RET_D
OD_Q, OD_K, OD_V, OD_R, OD_END = 0, 512, 1024, 2048, 3072

ROWS = 256
SUB_BLOCKS = 2
STEP_ROWS = SUB_BLOCKS * ROWS
FF_CHUNK = 1024
SWA_HEADS_PER_MLP_STEP = 4
EVEN_OUT_PIECES = 1
ODD_OUT_PIECES = 2
VMEM_LIMIT = 56 * 1024 * 1024

SWA_STACK_HEADS = tuple(h for p in range(SWA_GROUP) for h in (p, SWA_GROUP + p))


def _rms(x, g):
    return x * lax.rsqrt(jnp.mean(x * x, axis=-1, keepdims=True) + NORM_EPS) * g


def _silu(x):
    return x * (1.0 / (1.0 + jnp.exp(-x)))


def _dot(a, b):
    return jnp.dot(a.astype(BF16), b.astype(BF16), preferred_element_type=F32)


def _dot_nt(a, b):
    return lax.dot_general(a.astype(BF16), b.astype(BF16), (((1,), (1,)), ((), ())),
                           preferred_element_type=F32)


def _dot_tn(a, b):
    return lax.dot_general(a.astype(BF16), b.astype(BF16), (((0,), (0,)), ((), ())),
                           preferred_element_type=F32)


def _mlp_steps(x, gpre_ref, wup_ref, wdown_ref, gpost_ref, o_ref):
    u = _rms(x, gpre_ref[...]).astype(BF16)
    yield
    acc = jnp.zeros(x.shape, F32)
    n_chunks = D_FF // FF_CHUNK
    for c in range(n_chunks):
        cols = slice(c * FF_CHUNK, (c + 1) * FF_CHUNK)
        hid = jnp.maximum(jnp.dot(u, wup_ref[:, cols], preferred_element_type=F32), 0.0)
        hid = (hid * hid).astype(BF16)
        yield
        acc = acc + jnp.dot(hid, wdown_ref[cols, :], preferred_element_type=F32)
        if c == n_chunks - 1:
            o_ref[...] = x + _rms(acc, gpost_ref[...])
        yield


def _out_proj(x, mix_ref, wout_ref, gpost_ref, n_pieces):
    pieces = []
    for p in range(n_pieces):
        rows = slice(p * STEP_ROWS // n_pieces, (p + 1) * STEP_ROWS // n_pieces)
        out = jnp.dot(mix_ref[rows, :], wout_ref[...], preferred_element_type=F32)
        pieces.append(x[rows] + _rms(out, gpost_ref[...]))
    return jnp.concatenate(pieces, axis=0)


def _sub_block_rows():
    return [slice(sb * ROWS, (sb + 1) * ROWS) for sb in range(SUB_BLOCKS)]


def _retention_swa(mlp, first_in_seq, rq, rk, rv, rg, aq, ak, av, cos, sin, sinks_ref, dec_ref, qdec_ref,
                   kdec_ref, cdec_ref, rng_ref, state_ref, kprev_ref, vprev_ref, mix_ref):
    heads = range(RET_HEADS)
    head_cols = [slice(hd * RET_D, (hd + 1) * RET_D) for hd in heads]
    qs = [rq[:, c] * cos + pltpu.roll(rq[:, c], RET_D // 2, 1) * sin for c in head_cols]
    ks = [(rk[:, c] * cos + pltpu.roll(rk[:, c], RET_D // 2, 1) * sin) * (RET_D ** -0.5)
          for c in head_cols]
    vs = [rv[:, c].astype(BF16) for c in head_cols]
    states = [state_ref[hd] for hd in heads]
    scores = [_dot_nt(qs[hd], ks[hd]) * dec_ref[hd] for hd in heads]
    cross = [_dot(qs[hd] * qdec_ref[hd], states[hd]) for hd in heads]
    for hd in heads:
        state_ref[hd] = cdec_ref[hd] * states[hd] + _dot_tn(ks[hd] * kdec_ref[hd], vs[hd])
    outs = [_dot(scores[hd], vs[hd]) + cross[hd] for hd in heads]
    for hd in heads:
        o = outs[hd]
        y = o * lax.rsqrt(jnp.mean(o * o, axis=-1, keepdims=True) + NORM_EPS)
        mix_ref[:, head_cols[hd]] = (y * rng_ref[:, head_cols[hd]] * _silu(rg[:, head_cols[hd]])).astype(BF16)

    low = lax.broadcasted_iota(jnp.int32, (WINDOW, LANES), 1) < SWA_HD
    qi = lax.broadcasted_iota(jnp.int32, (WINDOW, 2 * WINDOW), 0)
    ki = lax.broadcasted_iota(jnp.int32, (WINDOW, 2 * WINDOW), 1)
    allowed = (ki > qi) & (ki <= qi + WINDOW)
    first_mask = allowed & (ki >= jnp.where(first_in_seq, WINDOW, 0))
    for c in range(ROWS // WINDOW):
        rows = slice(c * WINDOW, (c + 1) * WINDOW)
        if c == 0:
            kp, vp = kprev_ref[...], vprev_ref[...]
            mask = first_mask
        else:
            prev = slice((c - 1) * WINDOW, c * WINDOW)
            kp, vp = ak[prev], av[prev]
            mask = allowed
        kk = jnp.concatenate([kp, ak[rows]], axis=0)
        vv = jnp.concatenate([vp, av[rows]], axis=0)
        q_parts = []
        for p in range(SWA_GROUP):
            qg = aq[rows, p * LANES:(p + 1) * LANES]
            q_parts.append(jnp.where(low, qg, 0.0))
            q_parts.append(jnp.where(low, 0.0, qg))
        s_all = _dot_nt(jnp.concatenate(q_parts, axis=0), kk)
        ps = []
        for j, head in enumerate(SWA_STACK_HEADS):
            s = jnp.where(mask, s_all[j * WINDOW:(j + 1) * WINDOW], NEG_INF)
            sink = sinks_ref[head]
            m = jnp.maximum(jnp.max(s, axis=-1, keepdims=True), sink)
            p_un = jnp.exp(s - m)
            denom = jnp.sum(p_un, axis=-1, keepdims=True) + jnp.exp(sink - m)
            ps.append((p_un * (1.0 / denom)).astype(BF16))
            if j % SWA_HEADS_PER_MLP_STEP == SWA_HEADS_PER_MLP_STEP - 1:
                next(mlp, None)
        o_all = _dot(jnp.concatenate(ps, axis=0), vv)
        for p in range(SWA_GROUP):
            o_lo = o_all[(2 * p) * WINDOW:(2 * p + 1) * WINDOW]
            o_hi = o_all[(2 * p + 1) * WINDOW:(2 * p + 2) * WINDOW]
            mix_ref[rows, EV_RET_MIX + p * LANES:EV_RET_MIX + (p + 1) * LANES] = (
                jnp.where(low, o_lo, o_hi).astype(BF16))
    kprev_ref[...] = ak[ROWS - WINDOW:]
    vprev_ref[...] = av[ROWS - WINDOW:]


def _even_kernel(steps_per_seq, sinks_ref, h_ref, gpre_ref, win_ref, cos_ref, sin_ref, dec_ref,
                 qdec_ref, kdec_ref, cdec_ref, rng_ref, wout_ref, gpost_ref,
                 gmlp_ref, wup_ref, wdown_ref, gmlp_post_ref, o_ref,
                 state_ref, kprev_ref, vprev_ref, mix_ref, h1_ref):
    step = pl.program_id(0)
    first_step_of_seq = step % steps_per_seq == 0

    @pl.when(step == 0)
    def _():
        h1_ref[...] = jnp.zeros_like(h1_ref)

    @pl.when(first_step_of_seq)
    def _():
        state_ref[...] = jnp.zeros_like(state_ref)
        kprev_ref[...] = jnp.zeros_like(kprev_ref)
        vprev_ref[...] = jnp.zeros_like(vprev_ref)

    mlp = _mlp_steps(h1_ref[(step + 1) % 2], gmlp_ref, wup_ref, wdown_ref, gmlp_post_ref, o_ref)
    x = h_ref[...]
    u = _rms(x, gpre_ref[...]).astype(BF16)
    next(mlp, None)

    def proj(a, b):
        return jnp.dot(u, win_ref[:, a:b], preferred_element_type=F32)

    rq = proj(EV_RQ, EV_RK)
    rk = proj(EV_RK, EV_RV)
    rv = proj(EV_RV, EV_RG)
    rg = proj(EV_RG, EV_AQ)
    aq = proj(EV_AQ, EV_AK) * (SWA_HD ** -0.5)
    ak = proj(EV_AK, EV_AV)
    av = proj(EV_AV, EV_END)
    for sb, rows in enumerate(_sub_block_rows()):
        _retention_swa(
            mlp, first_step_of_seq if sb == 0 else False, rq[rows], rk[rows], rv[rows], rg[rows],
            aq[rows], ak[rows], av[rows], cos_ref[rows, :], sin_ref[rows, :], sinks_ref, dec_ref,
            qdec_ref, kdec_ref, cdec_ref, rng_ref, state_ref, kprev_ref, vprev_ref, mix_ref.at[rows])
    for _ in mlp:
        pass
    h1_ref[step % 2] = _out_proj(x, mix_ref, wout_ref, gpost_ref, EVEN_OUT_PIECES)


def _resident(shape):
    zeros = (0,) * len(shape)
    return pl.BlockSpec(shape, lambda s: zeros, pipeline_mode=pl.Buffered(1))


def _stacked(shape, index):
    zeros = (0,) * len(shape)
    return pl.BlockSpec((None,) + tuple(shape), lambda s: (index,) + zeros, pipeline_mode=pl.Buffered(1))


def _row_specs(n_blocks):
    in_spec = pl.BlockSpec((STEP_ROWS, D_MODEL), lambda s: (jnp.minimum(s, n_blocks - 1), 0))
    out_spec = pl.BlockSpec((STEP_ROWS, D_MODEL), lambda s: (jnp.maximum(s - 1, 0), 0))
    return in_spec, out_spec


def _layer_scratch():
    return [pltpu.VMEM((STEP_ROWS, D_MODEL), BF16),
            pltpu.VMEM((2, STEP_ROWS, D_MODEL), F32)]


def _even_layer(h, sinks, g_pre, w_in, cos2, sin2, dec, qdec, kdec, cdec, ret_g, w_out, g_post,
                g_mlp, w_up, w_down, g_mlp_post, seq, layer):
    nt = seq // STEP_ROWS
    n_blocks = h.shape[0] // STEP_ROWS
    in_spec, out_spec = _row_specs(n_blocks)
    pos_spec = pl.BlockSpec((STEP_ROWS, LANES), lambda s: (jnp.minimum(s, n_blocks - 1) % nt, 0))
    return pl.pallas_call(
        functools.partial(_even_kernel, nt),
        out_shape=jax.ShapeDtypeStruct(h.shape, h.dtype),
        grid=(n_blocks + 1,),
        in_specs=[
            pl.BlockSpec(memory_space=pltpu.SMEM),
            in_spec,
            _resident((1, D_MODEL)),
            _stacked((D_MODEL, EV_END), layer // 2),
            pos_spec, pos_spec,
            _resident((RET_HEADS, ROWS, ROWS)),
            _resident((RET_HEADS, ROWS, RET_D)),
            _resident((RET_HEADS, ROWS, RET_D)),
            _resident((RET_HEADS, RET_D, RET_D)),
            _resident((1, RET_HEADS * RET_D)),
            _stacked((D_MODEL, D_MODEL), layer // 2),
            _resident((1, D_MODEL)),
            _resident((1, D_MODEL)),
            _stacked((D_MODEL, D_FF), layer),
            _stacked((D_FF, D_MODEL), layer),
            _resident((1, D_MODEL)),
        ],
        out_specs=out_spec,
        scratch_shapes=[
            pltpu.VMEM((RET_HEADS, RET_D, RET_D), F32),
            pltpu.VMEM((WINDOW, LANES), F32),
            pltpu.VMEM((WINDOW, LANES), F32),
        ] + _layer_scratch(),
        compiler_params=pltpu.CompilerParams(
            dimension_semantics=("arbitrary",), vmem_limit_bytes=VMEM_LIMIT),
        name="even_layer",
    )(sinks, h, g_pre, w_in, cos2, sin2, dec, qdec, kdec, cdec, ret_g, w_out, g_post,
      g_mlp, w_up, w_down, g_mlp_post)


def _gla(mlp, q, k, v, r, lg_hi, lg_lo, tri_ref, ng_ref, state_ref, mix_ref):
    n_chunks = ROWS // GLA_CHUNK
    chunk_rows = [slice(c * GLA_CHUNK, (c + 1) * GLA_CHUNK) for c in range(n_chunks)]
    tri = tri_ref[...]
    b = (jnp.dot(tri, lg_hi, preferred_element_type=F32)
         + jnp.dot(tri, lg_lo, preferred_element_type=F32))
    next(mlp, None)
    b_last = [b[rows.stop - 1:rows.stop, :] for rows in chunk_rows]
    b_last_full = jnp.concatenate(
        [jnp.broadcast_to(bl, (GLA_CHUNK, bl.shape[1])) for bl in b_last], axis=0)
    q_in = q * jnp.exp(b)
    k_in = k * jnp.exp(-b)
    k_st = k * jnp.exp(b_last_full - b)

    ri = lax.broadcasted_iota(jnp.int32, (ROWS, ROWS), 0)
    ci = lax.broadcasted_iota(jnp.int32, (ROWS, ROWS), 1)
    causal = (ci <= ri) & (ci >= ri - (ri & (GLA_CHUNK - 1)))
    row_id = lax.broadcasted_iota(jnp.int32, (SUBLANES, GLA_DK), 0)
    heads = range(GLA_HEADS)
    kcs = [slice(hd * GLA_DK, (hd + 1) * GLA_DK) for hd in heads]
    vcs = [slice(hd * GLA_DV, (hd + 1) * GLA_DV) for hd in heads]
    scores = [jnp.where(causal, _dot_nt(q_in[:, kcs[hd]], k_in[:, kcs[hd]]), 0.0) for hd in heads]
    kvs = [[_dot_tn(k_st[rows, kcs[hd]], v[rows, vcs[hd]]) for rows in chunk_rows] for hd in heads]
    next(mlp, None)
    crosses = []
    for hd in heads:
        tile = jnp.zeros((SUBLANES, GLA_DK), F32)
        for c in range(n_chunks):
            tile = jnp.where(row_id == c, b_last[c][:, kcs[hd]], tile)
        tile = jnp.concatenate([tile, jnp.zeros((GLA_DK - SUBLANES, GLA_DK), F32)], axis=0).T
        state = state_ref[hd]
        cross = []
        for c in range(n_chunks):
            cross.append(_dot(q_in[chunk_rows[c], kcs[hd]], state))
            state = jnp.exp(tile[:, c:c + 1]) * state + kvs[hd][c]
        state_ref[hd] = state
        crosses.append(jnp.concatenate(cross, axis=0))
    next(mlp, None)
    outs = [_dot(scores[hd], v[:, vcs[hd]]) + crosses[hd] for hd in heads]
    next(mlp, None)
    for hd in heads:
        o = outs[hd]
        y = o * lax.rsqrt(jnp.mean(o * o, axis=-1, keepdims=True) + NORM_EPS)
        mix_ref[:, vcs[hd]] = (y * ng_ref[:, vcs[hd]] * _silu(r[:, vcs[hd]])).astype(BF16)


def _odd_kernel(steps_per_seq, h_ref, gpre_ref, win_ref, wlow_ref, wgu_ref, gbias_ref, tri_ref,
                ng_ref, wout_ref, gpost_ref, gmlp_ref, wup_ref, wdown_ref, gmlp_post_ref, o_ref,
                state_ref, mix_ref, h1_ref):
    step = pl.program_id(0)

    @pl.when(step == 0)
    def _():
        h1_ref[...] = jnp.zeros_like(h1_ref)

    @pl.when(step % steps_per_seq == 0)
    def _():
        state_ref[...] = jnp.zeros_like(state_ref)

    mlp = _mlp_steps(h1_ref[(step + 1) % 2], gmlp_ref, wup_ref, wdown_ref, gmlp_post_ref, o_ref)
    x = h_ref[...]
    u = _rms(x, gpre_ref[...]).astype(BF16)
    next(mlp, None)

    def proj(a, b):
        return jnp.dot(u, win_ref[:, a:b], preferred_element_type=F32)

    a_low = jnp.dot(u, wlow_ref[...], preferred_element_type=F32)
    k = proj(OD_K, OD_V)
    gl = _dot(a_low, wgu_ref[...]) + gbias_ref[...]
    q = proj(OD_Q, OD_K) * (GLA_DK ** -0.5)
    log_g = (jnp.minimum(gl, 0.0) - jnp.log1p(jnp.exp(-jnp.abs(gl)))) * (1.0 / GLA_TAU)
    lg_hi = log_g.astype(BF16)
    lg_lo = (log_g - lg_hi.astype(F32)).astype(BF16)
    v = proj(OD_V, OD_R).astype(BF16)
    r = proj(OD_R, OD_END)
    for rows in _sub_block_rows():
        _gla(mlp, q[rows], k[rows], v[rows], r[rows], lg_hi[rows], lg_lo[rows], tri_ref, ng_ref,
             state_ref, mix_ref.at[rows])
    for _ in mlp:
        pass
    h1_ref[step % 2] = _out_proj(x, mix_ref, wout_ref, gpost_ref, ODD_OUT_PIECES)


def _odd_layer(h, g_pre, w_in, w_low, w_gu, g_bias, tri, ng, w_out, g_post,
               g_mlp, w_up, w_down, g_mlp_post, seq, layer):
    nt = seq // STEP_ROWS
    n_blocks = h.shape[0] // STEP_ROWS
    in_spec, out_spec = _row_specs(n_blocks)
    return pl.pallas_call(
        functools.partial(_odd_kernel, nt),
        out_shape=jax.ShapeDtypeStruct(h.shape, h.dtype),
        grid=(n_blocks + 1,),
        in_specs=[
            in_spec,
            _resident((1, D_MODEL)),
            _stacked((D_MODEL, OD_END), layer // 2),
            _stacked((D_MODEL, RANK_PAD), layer // 2),
            _stacked((RANK_PAD, GLA_HEADS * GLA_DK), layer // 2),
            _resident((1, GLA_HEADS * GLA_DK)),
            _resident((ROWS, ROWS)),
            _resident((1, D_MODEL)),
            _stacked((D_MODEL, D_MODEL), layer // 2),
            _resident((1, D_MODEL)),
            _resident((1, D_MODEL)),
            _stacked((D_MODEL, D_FF), layer),
            _stacked((D_FF, D_MODEL), layer),
            _resident((1, D_MODEL)),
        ],
        out_specs=out_spec,
        scratch_shapes=[pltpu.VMEM((GLA_HEADS, GLA_DK, GLA_DV), F32)] + _layer_scratch(),
        compiler_params=pltpu.CompilerParams(
            dimension_semantics=("arbitrary",), vmem_limit_bytes=VMEM_LIMIT),
        name="odd_layer",
    )(h, g_pre, w_in, w_low, w_gu, g_bias, tri, ng, w_out, g_post, g_mlp, w_up, w_down, g_mlp_post)


def _retention_tables(seq):
    heads = jnp.arange(RET_HEADS, dtype=F32)
    log_gamma = jnp.log1p(-(2.0 ** (-5.0 - heads)))
    idx = jnp.arange(ROWS, dtype=F32)
    rel = idx[:, None] - idx[None, :]
    causal = rel >= 0
    dec = jnp.where(causal[None], jnp.exp(log_gamma[:, None, None] * jnp.where(causal, rel, 0.0)[None]), 0.0)
    k_dec = jnp.exp(log_gamma[None, :] * (ROWS - 1 - idx)[:, None])
    q_dec = jnp.exp(log_gamma[None, :] * (idx + 1.0)[:, None])
    c_dec = jnp.exp(log_gamma * ROWS)
    shape = (RET_HEADS, ROWS, RET_D)
    kdec = jnp.broadcast_to(k_dec.T[:, :, None], shape)
    qdec = jnp.broadcast_to(q_dec.T[:, :, None], shape)
    cdec = jnp.broadcast_to(c_dec[:, None, None], (RET_HEADS, RET_D, RET_D))
    half = RET_D // 2
    inv = ROPE_BASE ** (-jnp.arange(half, dtype=F32) / half)
    ang = jnp.arange(seq, dtype=jnp.int32).astype(F32)[:, None] * inv[None, :]
    cos, sin = jnp.cos(ang), jnp.sin(ang)
    cos2 = jnp.concatenate([cos, cos], axis=-1)
    sin2 = jnp.concatenate([-sin, sin], axis=-1)
    return dec, qdec, kdec, cdec, cos2, sin2


def kernel(x, norm_g, w_up, w_down, ev_w_in, ev_ret_norm_g, ev_sinks, ev_w_out, od_w_in,
           od_w_gate_up, od_gate_bias, od_norm_g, od_w_out):
    batch, seq, d = x.shape
    depth = norm_g.shape[0]
    h = x.reshape(batch * seq, d)
    dec, qdec, kdec, cdec, cos2, sin2 = _retention_tables(seq)
    pos = np.arange(ROWS)
    tri = jnp.asarray((pos[None, :] <= pos[:, None])
                      & (pos[None, :] // GLA_CHUNK == pos[:, None] // GLA_CHUNK), BF16)
    w_up_b, w_down_b = w_up.astype(BF16), w_down.astype(BF16)
    n_ev = ev_w_in.shape[0]
    aq_cols = ev_w_in[:, :, EV_AQ:EV_AK].reshape(n_ev, d, SWA_KV_HEADS, SWA_GROUP, SWA_HD)
    aq_cols = aq_cols.transpose(0, 1, 3, 2, 4).reshape(n_ev, d, EV_AK - EV_AQ)
    ev_w_in_b = jnp.concatenate(
        [ev_w_in[:, :, :EV_AQ], aq_cols, ev_w_in[:, :, EV_AK:]], axis=-1).astype(BF16)
    att_rows = ev_w_out[:, EV_RET_MIX:, :].reshape(n_ev, SWA_KV_HEADS, SWA_GROUP, SWA_HD, d)
    att_rows = att_rows.transpose(0, 2, 1, 3, 4).reshape(n_ev, SWA_HEADS * SWA_HD, d)
    ev_w_out_b = jnp.concatenate([ev_w_out[:, :EV_RET_MIX, :], att_rows], axis=1).astype(BF16)
    od_w_in_b = od_w_in.astype(BF16)
    od_w_low_b = jnp.pad(od_w_in[:, :, OD_END:], ((0, 0), (0, 0), (0, RANK_PAD - GLA_RANK))).astype(BF16)
    od_w_gu_b = jnp.pad(od_w_gate_up, ((0, 0), (0, RANK_PAD - GLA_RANK), (0, 0))).astype(BF16)
    od_w_out_b = od_w_out.astype(BF16)
    g = lambda layer, j: norm_g[layer, j].reshape(1, d)
    for layer in range(depth):
        i = layer // 2
        mlp_args = (g(layer, 2), w_up_b, w_down_b, g(layer, 3), seq, layer)
        if layer % 2 == 0:
            h = _even_layer(h, ev_sinks[i], g(layer, 0), ev_w_in_b, cos2, sin2, dec, qdec, kdec, cdec,
                            ev_ret_norm_g[i].reshape(1, -1), ev_w_out_b, g(layer, 1), *mlp_args)
        else:
            h = _odd_layer(h, g(layer, 0), od_w_in_b, od_w_low_b, od_w_gu_b,
                           od_gate_bias[i].reshape(1, -1), tri, od_norm_g[i].reshape(1, -1),
                           od_w_out_b, g(layer, 1), *mlp_args)
    return h.reshape(batch, seq, d)
```

```python
import functools

import numpy as np
import jax
import jax.numpy as jnp
from jax import lax
from jax.experimental import pallas as pl
from jax.experimental.pallas import tpu as pltpu

F32 = jnp.float32
BF16 = jnp.bfloat16

D_MODEL = 1024
D_FF = 4 * D_MODEL
NORM_EPS = 1e-6
NEG_INF = -1e30

RET_HEADS = 4
RET_D = 128
ROPE_BASE = 10000.0

SWA_HEADS = 8
SWA_KV_HEADS = 2
SWA_GROUP = SWA_HEADS // SWA_KV_HEADS
SWA_HD = 64
WINDOW = 128

GLA_HEADS = 4
GLA_DK = 128
GLA_DV = 256
GLA_RANK = 16
GLA_TAU = 16.0
GLA_CHUNK = 64

LANES = 128
SUBLANES = 8
RANK_PAD = LANES

EV_RQ, EV_RK, EV_RV, EV_RG, EV_AQ, EV_AK, EV_AV, EV_END = 0, 512, 1024, 1536, 2048, 2560, 2688, 2816
EV_RET_MIX = RET_HEADS * RET_D
OD_Q, OD_K, OD_V, OD_R, OD_END = 0, 512, 1024, 2048, 3072

ROWS = 256
SUB_BLOCKS = 2
STEP_ROWS = SUB_BLOCKS * ROWS
FF_CHUNK = 1024
SWA_HEADS_PER_MLP_STEP = 4
EVEN_OUT_PIECES = 1
ODD_OUT_PIECES = 2
VMEM_LIMIT = 56 * 1024 * 1024

SWA_STACK_HEADS = tuple(h for p in range(SWA_GROUP) for h in (p, SWA_GROUP + p))


def _rms(x, g):
    return x * lax.rsqrt(jnp.mean(x * x, axis=-1, keepdims=True) + NORM_EPS) * g


def _silu(x):
    return x * (1.0 / (1.0 + jnp.exp(-x)))


def _dot(a, b):
    return jnp.dot(a.astype(BF16), b.astype(BF16), preferred_element_type=F32)


def _dot_nt(a, b):
    return lax.dot_general(a.astype(BF16), b.astype(BF16), (((1,), (1,)), ((), ())),
                           preferred_element_type=F32)


def _dot_tn(a, b):
    return lax.dot_general(a.astype(BF16), b.astype(BF16), (((0,), (0,)), ((), ())),
                           preferred_element_type=F32)


def _mlp_steps(x, gpre_ref, wup_ref, wdown_ref, gpost_ref, o_ref):
    u = _rms(x, gpre_ref[...]).astype(BF16)
    yield
    acc = jnp.zeros(x.shape, F32)
    n_chunks = D_FF // FF_CHUNK
    for c in range(n_chunks):
        cols = slice(c * FF_CHUNK, (c + 1) * FF_CHUNK)
        hid = jnp.maximum(jnp.dot(u, wup_ref[:, cols], preferred_element_type=F32), 0.0)
        hid = (hid * hid).astype(BF16)
        yield
        acc = acc + jnp.dot(hid, wdown_ref[cols, :], preferred_element_type=F32)
        if c == n_chunks - 1:
            o_ref[...] = x + _rms(acc, gpost_ref[...])
        yield


def _out_proj(x, mix_ref, wout_ref, gpost_ref, n_pieces):
    pieces = []
    for p in range(n_pieces):
        rows = slice(p * STEP_ROWS // n_pieces, (p + 1) * STEP_ROWS // n_pieces)
        out = jnp.dot(mix_ref[rows, :], wout_ref[...], preferred_element_type=F32)
        pieces.append(x[rows] + _rms(out, gpost_ref[...]))
    return jnp.concatenate(pieces, axis=0)


def _sub_block_rows():
    return [slice(sb * ROWS, (sb + 1) * ROWS) for sb in range(SUB_BLOCKS)]


def _retention_swa(mlp, first_in_seq, rq, rk, rv, rg, aq, ak, av, cos, sin, sinks_ref, dec_ref, qdec_ref,
                   kdec_ref, cdec_ref, rng_ref, state_ref, kprev_ref, vprev_ref, mix_ref):
    heads = range(RET_HEADS)
    head_cols = [slice(hd * RET_D, (hd + 1) * RET_D) for hd in heads]
    qs = [rq[:, c] * cos + pltpu.roll(rq[:, c], RET_D // 2, 1) * sin for c in head_cols]
    ks = [(rk[:, c] * cos + pltpu.roll(rk[:, c], RET_D // 2, 1) * sin) * (RET_D ** -0.5)
          for c in head_cols]
    vs = [rv[:, c].astype(BF16) for c in head_cols]
    states = [state_ref[hd] for hd in heads]
    scores = [_dot_nt(qs[hd], ks[hd]) * dec_ref[hd] for hd in heads]
    cross = [_dot(qs[hd] * qdec_ref[hd], states[hd]) for hd in heads]
    for hd in heads:
        state_ref[hd] = cdec_ref[hd] * states[hd] + _dot_tn(ks[hd] * kdec_ref[hd], vs[hd])
    outs = [_dot(scores[hd], vs[hd]) + cross[hd] for hd in heads]
    for hd in heads:
        o = outs[hd]
        y = o * lax.rsqrt(jnp.mean(o * o, axis=-1, keepdims=True) + NORM_EPS)
        mix_ref[:, head_cols[hd]] = (y * rng_ref[:, head_cols[hd]] * _silu(rg[:, head_cols[hd]])).astype(BF16)

    low = lax.broadcasted_iota(jnp.int32, (WINDOW, LANES), 1) < SWA_HD
    qi = lax.broadcasted_iota(jnp.int32, (WINDOW, 2 * WINDOW), 0)
    ki = lax.broadcasted_iota(jnp.int32, (WINDOW, 2 * WINDOW), 1)
    allowed = (ki > qi) & (ki <= qi + WINDOW)
    first_mask = allowed & (ki >= jnp.where(first_in_seq, WINDOW, 0))
    for c in range(ROWS // WINDOW):
        rows = slice(c * WINDOW, (c + 1) * WINDOW)
        if c == 0:
            kp, vp = kprev_ref[...], vprev_ref[...]
            mask = first_mask
        else:
            prev = slice((c - 1) * WINDOW, c * WINDOW)
            kp, vp = ak[prev], av[prev]
            mask = allowed
        kk = jnp.concatenate([kp, ak[rows]], axis=0)
        vv = jnp.concatenate([vp, av[rows]], axis=0)
        q_parts = []
        for p in range(SWA_GROUP):
            qg = aq[rows, p * LANES:(p + 1) * LANES]
            q_parts.append(jnp.where(low, qg, 0.0))
            q_parts.append(jnp.where(low, 0.0, qg))
        s_all = _dot_nt(jnp.concatenate(q_parts, axis=0), kk)
        ps = []
        for j, head in enumerate(SWA_STACK_HEADS):
            s = jnp.where(mask, s_all[j * WINDOW:(j + 1) * WINDOW], NEG_INF)
            sink = sinks_ref[head]
            m = jnp.maximum(jnp.max(s, axis=-1, keepdims=True), sink)
            p_un = jnp.exp(s - m)
            denom = jnp.sum(p_un, axis=-1, keepdims=True) + jnp.exp(sink - m)
            ps.append((p_un * (1.0 / denom)).astype(BF16))
            if j % SWA_HEADS_PER_MLP_STEP == SWA_HEADS_PER_MLP_STEP - 1:
                next(mlp, None)
        o_all = _dot(jnp.concatenate(ps, axis=0), vv)
        for p in range(SWA_GROUP):
            o_lo = o_all[(2 * p) * WINDOW:(2 * p + 1) * WINDOW]
            o_hi = o_all[(2 * p + 1) * WINDOW:(2 * p + 2) * WINDOW]
            mix_ref[rows, EV_RET_MIX + p * LANES:EV_RET_MIX + (p + 1) * LANES] = (
                jnp.where(low, o_lo, o_hi).astype(BF16))
    kprev_ref[...] = ak[ROWS - WINDOW:]
    vprev_ref[...] = av[ROWS - WINDOW:]


EVEN_INPUTS = 17
ODD_INPUTS = 14


def _split_refs(refs, n_inputs, n_side):
    a, b, c = n_inputs, n_inputs + n_side, n_inputs + n_side + 1
    return refs[:a], refs[a:b], refs[b], refs[c:c + n_side], refs[c + n_side:]


def _even_kernel(steps_per_seq, n_side, *refs):
    inputs, side_src, o_ref, side_dst, scratch = _split_refs(refs, EVEN_INPUTS, n_side)
    (sinks_ref, h_ref, gpre_ref, win_ref, cos_ref, sin_ref, dec_ref, qdec_ref, kdec_ref, cdec_ref,
     rng_ref, wout_ref, gpost_ref, gmlp_ref, wup_ref, wdown_ref, gmlp_post_ref) = inputs
    state_ref, kprev_ref, vprev_ref, mix_ref, h1_ref = scratch
    _side_cast(side_src, side_dst)
    step = pl.program_id(0)
    first_step_of_seq = step % steps_per_seq == 0

    @pl.when(step == 0)
    def _():
        h1_ref[...] = jnp.zeros_like(h1_ref)

    @pl.when(first_step_of_seq)
    def _():
        state_ref[...] = jnp.zeros_like(state_ref)
        kprev_ref[...] = jnp.zeros_like(kprev_ref)
        vprev_ref[...] = jnp.zeros_like(vprev_ref)

    mlp = _mlp_steps(h1_ref[(step + 1) % 2], gmlp_ref, wup_ref, wdown_ref, gmlp_post_ref, o_ref)
    x = h_ref[...]
    u = _rms(x, gpre_ref[...]).astype(BF16)
    next(mlp, None)

    def proj(a, b):
        return jnp.dot(u, win_ref[:, a:b], preferred_element_type=F32)

    rq = proj(EV_RQ, EV_RK)
    rk = proj(EV_RK, EV_RV)
    rv = proj(EV_RV, EV_RG)
    rg = proj(EV_RG, EV_AQ)
    aq = proj(EV_AQ, EV_AK) * (SWA_HD ** -0.5)
    ak = proj(EV_AK, EV_AV)
    av = proj(EV_AV, EV_END)
    for sb, rows in enumerate(_sub_block_rows()):
        _retention_swa(
            mlp, first_step_of_seq if sb == 0 else False, rq[rows], rk[rows], rv[rows], rg[rows],
            aq[rows], ak[rows], av[rows], cos_ref[rows, :], sin_ref[rows, :], sinks_ref, dec_ref,
            qdec_ref, kdec_ref, cdec_ref, rng_ref, state_ref, kprev_ref, vprev_ref, mix_ref.at[rows])
    for _ in mlp:
        pass
    h1_ref[step % 2] = _out_proj(x, mix_ref, wout_ref, gpost_ref, EVEN_OUT_PIECES)


def _resident(shape):
    zeros = (0,) * len(shape)
    return pl.BlockSpec(shape, lambda s: zeros, pipeline_mode=pl.Buffered(1))


def _stacked(shape, index):
    zeros = (0,) * len(shape)
    return pl.BlockSpec((None,) + tuple(shape), lambda s: (index,) + zeros, pipeline_mode=pl.Buffered(1))


def _weight_spec(w, index):
    if w.ndim == 2:
        return _resident(w.shape)
    return _stacked(w.shape[1:], index)


def _side_cast_specs(n_blocks, sources):
    in_specs, out_specs, out_shapes = [], [], []
    for w, index, cols in sources:
        slab_rows = w.shape[1] // n_blocks
        in_specs.append(pl.BlockSpec(
            (None, slab_rows, w.shape[2]),
            lambda s, index=index: (index, jnp.minimum(s, n_blocks - 1), 0)))
        out_specs.append(pl.BlockSpec((slab_rows, cols), lambda s: (jnp.minimum(s, n_blocks - 1), 0)))
        out_shapes.append(jax.ShapeDtypeStruct((w.shape[1], cols), BF16))
    return in_specs, out_specs, out_shapes


def _side_cast(src_refs, dst_refs):
    for src, dst in zip(src_refs, dst_refs):
        dst[...] = src[:, :dst.shape[1]].astype(BF16)


def _row_specs(n_blocks):
    in_spec = pl.BlockSpec((STEP_ROWS, D_MODEL), lambda s: (jnp.minimum(s, n_blocks - 1), 0))
    out_spec = pl.BlockSpec((STEP_ROWS, D_MODEL), lambda s: (jnp.maximum(s - 1, 0), 0))
    return in_spec, out_spec


def _layer_scratch():
    return [pltpu.VMEM((STEP_ROWS, D_MODEL), BF16),
            pltpu.VMEM((2, STEP_ROWS, D_MODEL), F32)]


def _even_layer(h, sinks, g_pre, w_in, cos2, sin2, dec, qdec, kdec, cdec, ret_g, w_out, g_post,
                g_mlp, w_up, w_down, g_mlp_post, seq, layer, side_sources):
    nt = seq // STEP_ROWS
    n_blocks = h.shape[0] // STEP_ROWS
    in_spec, out_spec = _row_specs(n_blocks)
    pos_spec = pl.BlockSpec((STEP_ROWS, LANES), lambda s: (jnp.minimum(s, n_blocks - 1) % nt, 0))
    side_in, side_out, side_shapes = _side_cast_specs(n_blocks, side_sources)
    return pl.pallas_call(
        functools.partial(_even_kernel, nt, len(side_sources)),
        out_shape=[jax.ShapeDtypeStruct(h.shape, h.dtype)] + side_shapes,
        grid=(n_blocks + 1,),
        in_specs=[
            pl.BlockSpec(memory_space=pltpu.SMEM),
            in_spec,
            _resident((1, D_MODEL)),
            _weight_spec(w_in, layer // 2),
            pos_spec, pos_spec,
            _resident((RET_HEADS, ROWS, ROWS)),
            _resident((RET_HEADS, ROWS, RET_D)),
            _resident((RET_HEADS, ROWS, RET_D)),
            _resident((RET_HEADS, RET_D, RET_D)),
            _resident((1, RET_HEADS * RET_D)),
            _weight_spec(w_out, layer // 2),
            _resident((1, D_MODEL)),
            _resident((1, D_MODEL)),
            _weight_spec(w_up, layer),
            _weight_spec(w_down, layer),
            _resident((1, D_MODEL)),
        ] + side_in,
        out_specs=[out_spec] + side_out,
        scratch_shapes=[
            pltpu.VMEM((RET_HEADS, RET_D, RET_D), F32),
            pltpu.VMEM((WINDOW, LANES), F32),
            pltpu.VMEM((WINDOW, LANES), F32),
        ] + _layer_scratch(),
        compiler_params=pltpu.CompilerParams(
            dimension_semantics=("arbitrary",), vmem_limit_bytes=VMEM_LIMIT),
        name="even_layer",
    )(sinks, h, g_pre, w_in, cos2, sin2, dec, qdec, kdec, cdec, ret_g, w_out, g_post,
      g_mlp, w_up, w_down, g_mlp_post, *[w for w, _, _ in side_sources])


def _gla(mlp, q, k, v, r, lg_hi, lg_lo, tri_ref, ng_ref, state_ref, mix_ref):
    n_chunks = ROWS // GLA_CHUNK
    chunk_rows = [slice(c * GLA_CHUNK, (c + 1) * GLA_CHUNK) for c in range(n_chunks)]
    tri = tri_ref[...]
    b = (jnp.dot(tri, lg_hi, preferred_element_type=F32)
         + jnp.dot(tri, lg_lo, preferred_element_type=F32))
    next(mlp, None)
    b_last = [b[rows.stop - 1:rows.stop, :] for rows in chunk_rows]
    b_last_full = jnp.concatenate(
        [jnp.broadcast_to(bl, (GLA_CHUNK, bl.shape[1])) for bl in b_last], axis=0)
    q_in = q * jnp.exp(b)
    k_in = k * jnp.exp(-b)
    k_st = k * jnp.exp(b_last_full - b)

    ri = lax.broadcasted_iota(jnp.int32, (ROWS, ROWS), 0)
    ci = lax.broadcasted_iota(jnp.int32, (ROWS, ROWS), 1)
    causal = (ci <= ri) & (ci >= ri - (ri & (GLA_CHUNK - 1)))
    row_id = lax.broadcasted_iota(jnp.int32, (SUBLANES, GLA_DK), 0)
    heads = range(GLA_HEADS)
    kcs = [slice(hd * GLA_DK, (hd + 1) * GLA_DK) for hd in heads]
    vcs = [slice(hd * GLA_DV, (hd + 1) * GLA_DV) for hd in heads]
    scores = [jnp.where(causal, _dot_nt(q_in[:, kcs[hd]], k_in[:, kcs[hd]]), 0.0) for hd in heads]
    kvs = [[_dot_tn(k_st[rows, kcs[hd]], v[rows, vcs[hd]]) for rows in chunk_rows] for hd in heads]
    next(mlp, None)
    crosses = []
    for hd in heads:
        tile = jnp.zeros((SUBLANES, GLA_DK), F32)
        for c in range(n_chunks):
            tile = jnp.where(row_id == c, b_last[c][:, kcs[hd]], tile)
        tile = jnp.concatenate([tile, jnp.zeros((GLA_DK - SUBLANES, GLA_DK), F32)], axis=0).T
        state = state_ref[hd]
        cross = []
        for c in range(n_chunks):
            cross.append(_dot(q_in[chunk_rows[c], kcs[hd]], state))
            state = jnp.exp(tile[:, c:c + 1]) * state + kvs[hd][c]
        state_ref[hd] = state
        crosses.append(jnp.concatenate(cross, axis=0))
    next(mlp, None)
    outs = [_dot(scores[hd], v[:, vcs[hd]]) + crosses[hd] for hd in heads]
    next(mlp, None)
    for hd in heads:
        o = outs[hd]
        y = o * lax.rsqrt(jnp.mean(o * o, axis=-1, keepdims=True) + NORM_EPS)
        mix_ref[:, vcs[hd]] = (y * ng_ref[:, vcs[hd]] * _silu(r[:, vcs[hd]])).astype(BF16)


def _odd_kernel(steps_per_seq, n_side, *refs):
    inputs, side_src, o_ref, side_dst, scratch = _split_refs(refs, ODD_INPUTS, n_side)
    (h_ref, gpre_ref, win_ref, wlow_ref, wgu_ref, gbias_ref, tri_ref, ng_ref, wout_ref, gpost_ref,
     gmlp_ref, wup_ref, wdown_ref, gmlp_post_ref) = inputs
    state_ref, mix_ref, h1_ref = scratch
    _side_cast(side_src, side_dst)
    step = pl.program_id(0)

    @pl.when(step == 0)
    def _():
        h1_ref[...] = jnp.zeros_like(h1_ref)

    @pl.when(step % steps_per_seq == 0)
    def _():
        state_ref[...] = jnp.zeros_like(state_ref)

    mlp = _mlp_steps(h1_ref[(step + 1) % 2], gmlp_ref, wup_ref, wdown_ref, gmlp_post_ref, o_ref)
    x = h_ref[...]
    u = _rms(x, gpre_ref[...]).astype(BF16)
    next(mlp, None)

    def proj(a, b):
        return jnp.dot(u, win_ref[:, a:b], preferred_element_type=F32)

    a_low = jnp.dot(u, wlow_ref[...], preferred_element_type=F32)
    k = proj(OD_K, OD_V)
    gl = _dot(a_low, wgu_ref[...]) + gbias_ref[...]
    q = proj(OD_Q, OD_K) * (GLA_DK ** -0.5)
    log_g = (jnp.minimum(gl, 0.0) - jnp.log1p(jnp.exp(-jnp.abs(gl)))) * (1.0 / GLA_TAU)
    lg_hi = log_g.astype(BF16)
    lg_lo = (log_g - lg_hi.astype(F32)).astype(BF16)
    v = proj(OD_V, OD_R).astype(BF16)
    r = proj(OD_R, OD_END)
    for rows in _sub_block_rows():
        _gla(mlp, q[rows], k[rows], v[rows], r[rows], lg_hi[rows], lg_lo[rows], tri_ref, ng_ref,
             state_ref, mix_ref.at[rows])
    for _ in mlp:
        pass
    h1_ref[step % 2] = _out_proj(x, mix_ref, wout_ref, gpost_ref, ODD_OUT_PIECES)


def _odd_layer(h, g_pre, w_in, w_low, w_gu, g_bias, tri, ng, w_out, g_post,
               g_mlp, w_up, w_down, g_mlp_post, seq, layer, side_sources):
    nt = seq // STEP_ROWS
    n_blocks = h.shape[0] // STEP_ROWS
    in_spec, out_spec = _row_specs(n_blocks)
    side_in, side_out, side_shapes = _side_cast_specs(n_blocks, side_sources)
    return pl.pallas_call(
        functools.partial(_odd_kernel, nt, len(side_sources)),
        out_shape=[jax.ShapeDtypeStruct(h.shape, h.dtype)] + side_shapes,
        grid=(n_blocks + 1,),
        in_specs=[
            in_spec,
            _resident((1, D_MODEL)),
            _weight_spec(w_in, layer // 2),
            _stacked((D_MODEL, RANK_PAD), layer // 2),
            _stacked((RANK_PAD, GLA_HEADS * GLA_DK), layer // 2),
            _resident((1, GLA_HEADS * GLA_DK)),
            _resident((ROWS, ROWS)),
            _resident((1, D_MODEL)),
            _weight_spec(w_out, layer // 2),
            _resident((1, D_MODEL)),
            _resident((1, D_MODEL)),
            _weight_spec(w_up, layer),
            _weight_spec(w_down, layer),
            _resident((1, D_MODEL)),
        ] + side_in,
        out_specs=[out_spec] + side_out,
        scratch_shapes=[pltpu.VMEM((GLA_HEADS, GLA_DK, GLA_DV), F32)] + _layer_scratch(),
        compiler_params=pltpu.CompilerParams(
            dimension_semantics=("arbitrary",), vmem_limit_bytes=VMEM_LIMIT),
        name="odd_layer",
    )(h, g_pre, w_in, w_low, w_gu, g_bias, tri, ng, w_out, g_post, g_mlp, w_up, w_down, g_mlp_post,
      *[w for w, _, _ in side_sources])


def _retention_tables(seq):
    heads = jnp.arange(RET_HEADS, dtype=F32)
    log_gamma = jnp.log1p(-(2.0 ** (-5.0 - heads)))
    idx = jnp.arange(ROWS, dtype=F32)
    rel = idx[:, None] - idx[None, :]
    causal = rel >= 0
    dec = jnp.where(causal[None], jnp.exp(log_gamma[:, None, None] * jnp.where(causal, rel, 0.0)[None]), 0.0)
    k_dec = jnp.exp(log_gamma[None, :] * (ROWS - 1 - idx)[:, None])
    q_dec = jnp.exp(log_gamma[None, :] * (idx + 1.0)[:, None])
    c_dec = jnp.exp(log_gamma * ROWS)
    shape = (RET_HEADS, ROWS, RET_D)
    kdec = jnp.broadcast_to(k_dec.T[:, :, None], shape)
    qdec = jnp.broadcast_to(q_dec.T[:, :, None], shape)
    cdec = jnp.broadcast_to(c_dec[:, None, None], (RET_HEADS, RET_D, RET_D))
    half = RET_D // 2
    inv = ROPE_BASE ** (-jnp.arange(half, dtype=F32) / half)
    ang = jnp.arange(seq, dtype=jnp.int32).astype(F32)[:, None] * inv[None, :]
    cos, sin = jnp.cos(ang), jnp.sin(ang)
    cos2 = jnp.concatenate([cos, cos], axis=-1)
    sin2 = jnp.concatenate([-sin, sin], axis=-1)
    return dec, qdec, kdec, cdec, cos2, sin2


def kernel(x, norm_g, w_up, w_down, ev_w_in, ev_ret_norm_g, ev_sinks, ev_w_out, od_w_in,
           od_w_gate_up, od_gate_bias, od_norm_g, od_w_out):
    batch, seq, d = x.shape
    depth = norm_g.shape[0]
    h = x.reshape(batch * seq, d)
    dec, qdec, kdec, cdec, cos2, sin2 = _retention_tables(seq)
    pos = np.arange(ROWS)
    tri = jnp.asarray((pos[None, :] <= pos[:, None])
                      & (pos[None, :] // GLA_CHUNK == pos[:, None] // GLA_CHUNK), BF16)
    w_up_b, w_down_b = w_up[0].astype(BF16), w_down[0].astype(BF16)
    od_w_out_b = None
    od_w_in_b = od_w_in.astype(BF16)
    n_ev = ev_w_in.shape[0]
    aq_cols = ev_w_in[:, :, EV_AQ:EV_AK].reshape(n_ev, d, SWA_KV_HEADS, SWA_GROUP, SWA_HD)
    aq_cols = aq_cols.transpose(0, 1, 3, 2, 4).reshape(n_ev, d, EV_AK - EV_AQ)
    ev_w_in_b = jnp.concatenate(
        [ev_w_in[:, :, :EV_AQ], aq_cols, ev_w_in[:, :, EV_AK:]], axis=-1).astype(BF16)
    att_rows = ev_w_out[:, EV_RET_MIX:, :].reshape(n_ev, SWA_KV_HEADS, SWA_GROUP, SWA_HD, d)
    att_rows = att_rows.transpose(0, 2, 1, 3, 4).reshape(n_ev, SWA_HEADS * SWA_HD, d)
    ev_w_out_b = jnp.concatenate([ev_w_out[:, :EV_RET_MIX, :], att_rows], axis=1).astype(BF16)
    od_w_low_b = jnp.pad(od_w_in[:, :, OD_END:], ((0, 0), (0, 0), (0, RANK_PAD - GLA_RANK))).astype(BF16)
    od_w_gu_b = jnp.pad(od_w_gate_up, ((0, 0), (0, RANK_PAD - GLA_RANK), (0, 0))).astype(BF16)
    g = lambda layer, j: norm_g[layer, j].reshape(1, d)
    for layer in range(depth):
        i = layer // 2
        nxt = layer + 1
        side = []
        if nxt < depth:
            side = [(w_up, nxt, D_FF), (w_down, nxt, d)]
            if nxt % 2 == 1:
                side += [(od_w_out, nxt // 2, d)]
        mlp_args = (g(layer, 2), w_up_b, w_down_b, g(layer, 3), seq, layer, side)
        if layer % 2 == 0:
            h, *cast = _even_layer(h, ev_sinks[i], g(layer, 0), ev_w_in_b, cos2, sin2, dec, qdec, kdec,
                                   cdec, ev_ret_norm_g[i].reshape(1, -1), ev_w_out_b, g(layer, 1),
                                   *mlp_args)
        else:
            h, *cast = _odd_layer(h, g(layer, 0), od_w_in_b, od_w_low_b, od_w_gu_b,
                                  od_gate_bias[i].reshape(1, -1), tri, od_norm_g[i].reshape(1, -1),
                                  od_w_out_b, g(layer, 1), *mlp_args)
        if cast:
            w_up_b, w_down_b = cast[:2]
        if len(cast) == 3:
            od_w_out_b = cast[2]
    return h.reshape(batch, seq, d)
```

```python
import functools

import numpy as np
import jax
import jax.numpy as jnp
from jax import lax
from jax.experimental import pallas as pl
from jax.experimental.pallas import tpu as pltpu

F32 = jnp.float32
BF16 = jnp.bfloat16

D_MODEL = 1024
D_FF = 4 * D_MODEL
NORM_EPS = 1e-6
NEG_INF = -1e30

RET_HEADS = 4
RET_D = 128
ROPE_BASE = 10000.0

SWA_HEADS = 8
SWA_KV_HEADS = 2
SWA_GROUP = SWA_HEADS // SWA_KV_HEADS
SWA_HD = 64
WINDOW = 128

GLA_HEADS = 4
GLA_DK = 128
GLA_DV = 256
GLA_RANK = 16
GLA_TAU = 16.0
GLA_CHUNK = 64

LANES = 128
SUBLANES = 8
RANK_PAD = LANES

EV_RQ, EV_RK, EV_RV, EV_RG, EV_AQ, EV_AK, EV_AV, EV_END = 0, 512, 1024, 1536, 2048, 2560, 2688, 2816
EV_RET_MIX = RET_HEADS * RET_D
OD_Q, OD_K, OD_V, OD_R, OD_END = 0, 512, 1024, 2048, 3072

ROWS = 256
SUB_BLOCKS = 2
STEP_ROWS = SUB_BLOCKS * ROWS
FF_CHUNK = 1024
SWA_HEADS_PER_MLP_STEP = 4
EVEN_OUT_PIECES = 1
ODD_OUT_PIECES = 4
VMEM_LIMIT = 56 * 1024 * 1024

SWA_STACK_HEADS = tuple(h for p in range(SWA_GROUP) for h in (p, SWA_GROUP + p))


def _rms(x, g):
    return x * lax.rsqrt(jnp.mean(x * x, axis=-1, keepdims=True) + NORM_EPS) * g


def _silu(x):
    return x * (1.0 / (1.0 + jnp.exp(-x)))


def _dot(a, b):
    return jnp.dot(a.astype(BF16), b.astype(BF16), preferred_element_type=F32)


def _dot_nt(a, b):
    return lax.dot_general(a.astype(BF16), b.astype(BF16), (((1,), (1,)), ((), ())),
                           preferred_element_type=F32)


def _dot_tn(a, b):
    return lax.dot_general(a.astype(BF16), b.astype(BF16), (((0,), (0,)), ((), ())),
                           preferred_element_type=F32)


def _mlp_steps(x, gpre_ref, wup_ref, wdown_ref, gpost_ref, o_ref):
    u = _rms(x, gpre_ref[...]).astype(BF16)
    yield
    acc = jnp.zeros(x.shape, F32)
    n_chunks = D_FF // FF_CHUNK
    for c in range(n_chunks):
        cols = slice(c * FF_CHUNK, (c + 1) * FF_CHUNK)
        hid = jnp.maximum(jnp.dot(u, wup_ref[:, cols], preferred_element_type=F32), 0.0)
        hid = (hid * hid).astype(BF16)
        yield
        acc = acc + jnp.dot(hid, wdown_ref[cols, :], preferred_element_type=F32)
        if c == n_chunks - 1:
            o_ref[...] = x + _rms(acc, gpost_ref[...])
        yield


def _out_proj(x, mix_ref, wout_ref, gpost_ref, n_pieces):
    pieces = []
    for p in range(n_pieces):
        rows = slice(p * STEP_ROWS // n_pieces, (p + 1) * STEP_ROWS // n_pieces)
        out = jnp.dot(mix_ref[rows, :], wout_ref[...], preferred_element_type=F32)
        pieces.append(x[rows] + _rms(out, gpost_ref[...]))
    return jnp.concatenate(pieces, axis=0)


def _sub_block_rows():
    return [slice(sb * ROWS, (sb + 1) * ROWS) for sb in range(SUB_BLOCKS)]


def _retention_swa(mlp, first_in_seq, rq, rk, rv, rg, aq, ak, av, cos, sin, sinks_ref, dec_ref, qdec_ref,
                   kdec_ref, cdec_ref, rng_ref, state_ref, kprev_ref, vprev_ref, mix_ref):
    heads = range(RET_HEADS)
    head_cols = [slice(hd * RET_D, (hd + 1) * RET_D) for hd in heads]
    qs = [rq[:, c] * cos + pltpu.roll(rq[:, c], RET_D // 2, 1) * sin for c in head_cols]
    ks = [(rk[:, c] * cos + pltpu.roll(rk[:, c], RET_D // 2, 1) * sin) * (RET_D ** -0.5)
          for c in head_cols]
    vs = [rv[:, c].astype(BF16) for c in head_cols]
    states = [state_ref[hd] for hd in heads]
    scores = [_dot_nt(qs[hd], ks[hd]) * dec_ref[hd] for hd in heads]
    cross = [_dot(qs[hd] * qdec_ref[hd], states[hd]) for hd in heads]
    for hd in heads:
        state_ref[hd] = cdec_ref[hd] * states[hd] + _dot_tn(ks[hd] * kdec_ref[hd], vs[hd])
    outs = [_dot(scores[hd], vs[hd]) + cross[hd] for hd in heads]
    for hd in heads:
        o = outs[hd]
        y = o * lax.rsqrt(jnp.mean(o * o, axis=-1, keepdims=True) + NORM_EPS)
        mix_ref[:, head_cols[hd]] = (y * rng_ref[:, head_cols[hd]] * _silu(rg[:, head_cols[hd]])).astype(BF16)

    low = lax.broadcasted_iota(jnp.int32, (WINDOW, LANES), 1) < SWA_HD
    qi = lax.broadcasted_iota(jnp.int32, (WINDOW, 2 * WINDOW), 0)
    ki = lax.broadcasted_iota(jnp.int32, (WINDOW, 2 * WINDOW), 1)
    allowed = (ki > qi) & (ki <= qi + WINDOW)
    first_mask = allowed & (ki >= jnp.where(first_in_seq, WINDOW, 0))
    for c in range(ROWS // WINDOW):
        rows = slice(c * WINDOW, (c + 1) * WINDOW)
        if c == 0:
            kp, vp = kprev_ref[...], vprev_ref[...]
            mask = first_mask
        else:
            prev = slice((c - 1) * WINDOW, c * WINDOW)
            kp, vp = ak[prev], av[prev]
            mask = allowed
        kk = jnp.concatenate([kp, ak[rows]], axis=0)
        vv = jnp.concatenate([vp, av[rows]], axis=0)
        q_parts = []
        for p in range(SWA_GROUP):
            qg = aq[rows, p * LANES:(p + 1) * LANES]
            q_parts.append(jnp.where(low, qg, 0.0))
            q_parts.append(jnp.where(low, 0.0, qg))
        s_all = _dot_nt(jnp.concatenate(q_parts, axis=0), kk)
        ps = []
        for j, head in enumerate(SWA_STACK_HEADS):
            s = jnp.where(mask, s_all[j * WINDOW:(j + 1) * WINDOW], NEG_INF)
            sink = sinks_ref[head]
            m = jnp.maximum(jnp.max(s, axis=-1, keepdims=True), sink)
            p_un = jnp.exp(s - m)
            denom = jnp.sum(p_un, axis=-1, keepdims=True) + jnp.exp(sink - m)
            ps.append((p_un * (1.0 / denom)).astype(BF16))
            if j % SWA_HEADS_PER_MLP_STEP == SWA_HEADS_PER_MLP_STEP - 1:
                next(mlp, None)
        o_all = _dot(jnp.concatenate(ps, axis=0), vv)
        for p in range(SWA_GROUP):
            o_lo = o_all[(2 * p) * WINDOW:(2 * p + 1) * WINDOW]
            o_hi = o_all[(2 * p + 1) * WINDOW:(2 * p + 2) * WINDOW]
            mix_ref[rows, EV_RET_MIX + p * LANES:EV_RET_MIX + (p + 1) * LANES] = (
                jnp.where(low, o_lo, o_hi).astype(BF16))
    kprev_ref[...] = ak[ROWS - WINDOW:]
    vprev_ref[...] = av[ROWS - WINDOW:]


EVEN_INPUTS = 17
ODD_INPUTS = 14


def _split_refs(refs, n_inputs, n_side):
    a, b, c = n_inputs, n_inputs + n_side, n_inputs + n_side + 1
    return refs[:a], refs[a:b], refs[b], refs[c:c + n_side], refs[c + n_side:]


def _even_kernel(steps_per_seq, n_side, *refs):
    inputs, side_src, o_ref, side_dst, scratch = _split_refs(refs, EVEN_INPUTS, n_side)
    (sinks_ref, h_ref, gpre_ref, win_ref, cos_ref, sin_ref, dec_ref, qdec_ref, kdec_ref, cdec_ref,
     rng_ref, wout_ref, gpost_ref, gmlp_ref, wup_ref, wdown_ref, gmlp_post_ref) = inputs
    state_ref, kprev_ref, vprev_ref, mix_ref, h1_ref = scratch
    _side_cast(side_src, side_dst)
    step = pl.program_id(0)
    first_step_of_seq = step % steps_per_seq == 0

    @pl.when(step == 0)
    def _():
        h1_ref[...] = jnp.zeros_like(h1_ref)

    @pl.when(first_step_of_seq)
    def _():
        state_ref[...] = jnp.zeros_like(state_ref)
        kprev_ref[...] = jnp.zeros_like(kprev_ref)
        vprev_ref[...] = jnp.zeros_like(vprev_ref)

    mlp = _mlp_steps(h1_ref[(step + 1) % 2], gmlp_ref, wup_ref, wdown_ref, gmlp_post_ref, o_ref)
    x = h_ref[...]
    u = _rms(x, gpre_ref[...]).astype(BF16)
    next(mlp, None)

    def proj(a, b):
        return jnp.dot(u, win_ref[:, a:b], preferred_element_type=F32)

    rq = proj(EV_RQ, EV_RK)
    rk = proj(EV_RK, EV_RV)
    rv = proj(EV_RV, EV_RG)
    rg = proj(EV_RG, EV_AQ)
    aq = proj(EV_AQ, EV_AK) * (SWA_HD ** -0.5)
    ak = proj(EV_AK, EV_AV)
    av = proj(EV_AV, EV_END)
    for sb, rows in enumerate(_sub_block_rows()):
        _retention_swa(
            mlp, first_step_of_seq if sb == 0 else False, rq[rows], rk[rows], rv[rows], rg[rows],
            aq[rows], ak[rows], av[rows], cos_ref[rows, :], sin_ref[rows, :], sinks_ref, dec_ref,
            qdec_ref, kdec_ref, cdec_ref, rng_ref, state_ref, kprev_ref, vprev_ref, mix_ref.at[rows])
    for _ in mlp:
        pass
    h1_ref[step % 2] = _out_proj(x, mix_ref, wout_ref, gpost_ref, EVEN_OUT_PIECES)


def _resident(shape):
    zeros = (0,) * len(shape)
    return pl.BlockSpec(shape, lambda s: zeros, pipeline_mode=pl.Buffered(1))


def _stacked(shape, index):
    zeros = (0,) * len(shape)
    return pl.BlockSpec((None,) + tuple(shape), lambda s: (index,) + zeros, pipeline_mode=pl.Buffered(1))


def _weight_spec(w, index):
    if w.ndim == 2:
        return _resident(w.shape)
    return _stacked(w.shape[1:], index)


def _side_cast_specs(n_blocks, sources):
    in_specs, out_specs, out_shapes = [], [], []
    for w, index, cols in sources:
        slab_rows = w.shape[1] // n_blocks
        in_specs.append(pl.BlockSpec(
            (None, slab_rows, w.shape[2]),
            lambda s, index=index: (index, jnp.minimum(s, n_blocks - 1), 0)))
        out_specs.append(pl.BlockSpec((slab_rows, cols), lambda s: (jnp.minimum(s, n_blocks - 1), 0)))
        out_shapes.append(jax.ShapeDtypeStruct((w.shape[1], cols), BF16))
    return in_specs, out_specs, out_shapes


def _side_cast(src_refs, dst_refs):
    for src, dst in zip(src_refs, dst_refs):
        dst[...] = src[:, :dst.shape[1]].astype(BF16)


def _row_specs(n_blocks):
    in_spec = pl.BlockSpec((STEP_ROWS, D_MODEL), lambda s: (jnp.minimum(s, n_blocks - 1), 0))
    out_spec = pl.BlockSpec((STEP_ROWS, D_MODEL), lambda s: (jnp.maximum(s - 1, 0), 0))
    return in_spec, out_spec


def _layer_scratch():
    return [pltpu.VMEM((STEP_ROWS, D_MODEL), BF16),
            pltpu.VMEM((2, STEP_ROWS, D_MODEL), F32)]


def _even_layer(h, sinks, g_pre, w_in, cos2, sin2, dec, qdec, kdec, cdec, ret_g, w_out, g_post,
                g_mlp, w_up, w_down, g_mlp_post, seq, layer, side_sources):
    nt = seq // STEP_ROWS
    n_blocks = h.shape[0] // STEP_ROWS
    in_spec, out_spec = _row_specs(n_blocks)
    pos_spec = pl.BlockSpec((STEP_ROWS, LANES), lambda s: (jnp.minimum(s, n_blocks - 1) % nt, 0))
    side_in, side_out, side_shapes = _side_cast_specs(n_blocks, side_sources)
    return pl.pallas_call(
        functools.partial(_even_kernel, nt, len(side_sources)),
        out_shape=[jax.ShapeDtypeStruct(h.shape, h.dtype)] + side_shapes,
        grid=(n_blocks + 1,),
        in_specs=[
            pl.BlockSpec(memory_space=pltpu.SMEM),
            in_spec,
            _resident((1, D_MODEL)),
            _weight_spec(w_in, layer // 2),
            pos_spec, pos_spec,
            _resident((RET_HEADS, ROWS, ROWS)),
            _resident((RET_HEADS, ROWS, RET_D)),
            _resident((RET_HEADS, ROWS, RET_D)),
            _resident((RET_HEADS, RET_D, RET_D)),
            _resident((1, RET_HEADS * RET_D)),
            _weight_spec(w_out, layer // 2),
            _resident((1, D_MODEL)),
            _resident((1, D_MODEL)),
            _weight_spec(w_up, layer),
            _weight_spec(w_down, layer),
            _resident((1, D_MODEL)),
        ] + side_in,
        out_specs=[out_spec] + side_out,
        scratch_shapes=[
            pltpu.VMEM((RET_HEADS, RET_D, RET_D), F32),
            pltpu.VMEM((WINDOW, LANES), F32),
            pltpu.VMEM((WINDOW, LANES), F32),
        ] + _layer_scratch(),
        compiler_params=pltpu.CompilerParams(
            dimension_semantics=("arbitrary",), vmem_limit_bytes=VMEM_LIMIT),
        name="even_layer",
    )(sinks, h, g_pre, w_in, cos2, sin2, dec, qdec, kdec, cdec, ret_g, w_out, g_post,
      g_mlp, w_up, w_down, g_mlp_post, *[w for w, _, _ in side_sources])


def _gla(mlp, q, k, v, r, lg_hi, lg_lo, tri_ref, ng_ref, state_ref, mix_ref):
    n_chunks = ROWS // GLA_CHUNK
    chunk_rows = [slice(c * GLA_CHUNK, (c + 1) * GLA_CHUNK) for c in range(n_chunks)]
    tri = tri_ref[...]
    b = (jnp.dot(tri, lg_hi, preferred_element_type=F32)
         + jnp.dot(tri, lg_lo, preferred_element_type=F32))
    next(mlp, None)
    b_last = [b[rows.stop - 1:rows.stop, :] for rows in chunk_rows]
    b_last_full = jnp.concatenate(
        [jnp.broadcast_to(bl, (GLA_CHUNK, bl.shape[1])) for bl in b_last], axis=0)
    q_in = q * jnp.exp(b)
    k_in = k * jnp.exp(-b)
    k_st = k * jnp.exp(b_last_full - b)

    ri = lax.broadcasted_iota(jnp.int32, (ROWS, ROWS), 0)
    ci = lax.broadcasted_iota(jnp.int32, (ROWS, ROWS), 1)
    causal = (ci <= ri) & (ci >= ri - (ri & (GLA_CHUNK - 1)))
    row_id = lax.broadcasted_iota(jnp.int32, (SUBLANES, GLA_DK), 0)
    heads = range(GLA_HEADS)
    kcs = [slice(hd * GLA_DK, (hd + 1) * GLA_DK) for hd in heads]
    vcs = [slice(hd * GLA_DV, (hd + 1) * GLA_DV) for hd in heads]
    scores = [jnp.where(causal, _dot_nt(q_in[:, kcs[hd]], k_in[:, kcs[hd]]), 0.0) for hd in heads]
    kvs = [[_dot_tn(k_st[rows, kcs[hd]], v[rows, vcs[hd]]) for rows in chunk_rows] for hd in heads]
    next(mlp, None)
    crosses = []
    for hd in heads:
        tile = jnp.zeros((SUBLANES, GLA_DK), F32)
        for c in range(n_chunks):
            tile = jnp.where(row_id == c, b_last[c][:, kcs[hd]], tile)
        tile = jnp.concatenate([tile, jnp.zeros((GLA_DK - SUBLANES, GLA_DK), F32)], axis=0).T
        state = state_ref[hd]
        cross = []
        for c in range(n_chunks):
            cross.append(_dot(q_in[chunk_rows[c], kcs[hd]], state))
            state = jnp.exp(tile[:, c:c + 1]) * state + kvs[hd][c]
        state_ref[hd] = state
        crosses.append(jnp.concatenate(cross, axis=0))
    next(mlp, None)
    outs = [_dot(scores[hd], v[:, vcs[hd]]) + crosses[hd] for hd in heads]
    next(mlp, None)
    for hd in heads:
        o = outs[hd]
        y = o * lax.rsqrt(jnp.mean(o * o, axis=-1, keepdims=True) + NORM_EPS)
        mix_ref[:, vcs[hd]] = (y * ng_ref[:, vcs[hd]] * _silu(r[:, vcs[hd]])).astype(BF16)


def _odd_kernel(steps_per_seq, n_side, *refs):
    inputs, side_src, o_ref, side_dst, scratch = _split_refs(refs, ODD_INPUTS, n_side)
    (h_ref, gpre_ref, win_ref, wlow_ref, wgu_ref, gbias_ref, tri_ref, ng_ref, wout_ref, gpost_ref,
     gmlp_ref, wup_ref, wdown_ref, gmlp_post_ref) = inputs
    state_ref, mix_ref, h1_ref = scratch
    _side_cast(side_src, side_dst)
    step = pl.program_id(0)

    @pl.when(step == 0)
    def _():
        h1_ref[...] = jnp.zeros_like(h1_ref)

    @pl.when(step % steps_per_seq == 0)
    def _():
        state_ref[...] = jnp.zeros_like(state_ref)

    mlp = _mlp_steps(h1_ref[(step + 1) % 2], gmlp_ref, wup_ref, wdown_ref, gmlp_post_ref, o_ref)
    x = h_ref[...]
    u = _rms(x, gpre_ref[...]).astype(BF16)
    next(mlp, None)

    def proj(a, b):
        return jnp.dot(u, win_ref[:, a:b], preferred_element_type=F32)

    a_low = jnp.dot(u, wlow_ref[...], preferred_element_type=F32)
    k = proj(OD_K, OD_V)
    gl = _dot(a_low, wgu_ref[...]) + gbias_ref[...]
    q = proj(OD_Q, OD_K) * (GLA_DK ** -0.5)
    log_g = (jnp.minimum(gl, 0.0) - jnp.log1p(jnp.exp(-jnp.abs(gl)))) * (1.0 / GLA_TAU)
    lg_hi = log_g.astype(BF16)
    lg_lo = (log_g - lg_hi.astype(F32)).astype(BF16)
    v = proj(OD_V, OD_R).astype(BF16)
    r = proj(OD_R, OD_END)
    for rows in _sub_block_rows():
        _gla(mlp, q[rows], k[rows], v[rows], r[rows], lg_hi[rows], lg_lo[rows], tri_ref, ng_ref,
             state_ref, mix_ref.at[rows])
    for _ in mlp:
        pass
    h1_ref[step % 2] = _out_proj(x, mix_ref, wout_ref, gpost_ref, ODD_OUT_PIECES)


def _odd_layer(h, g_pre, w_in, w_low, w_gu, g_bias, tri, ng, w_out, g_post,
               g_mlp, w_up, w_down, g_mlp_post, seq, layer, side_sources):
    nt = seq // STEP_ROWS
    n_blocks = h.shape[0] // STEP_ROWS
    in_spec, out_spec = _row_specs(n_blocks)
    side_in, side_out, side_shapes = _side_cast_specs(n_blocks, side_sources)
    return pl.pallas_call(
        functools.partial(_odd_kernel, nt, len(side_sources)),
        out_shape=[jax.ShapeDtypeStruct(h.shape, h.dtype)] + side_shapes,
        grid=(n_blocks + 1,),
        in_specs=[
            in_spec,
            _resident((1, D_MODEL)),
            _weight_spec(w_in, layer // 2),
            _stacked((D_MODEL, RANK_PAD), layer // 2),
            _stacked((RANK_PAD, GLA_HEADS * GLA_DK), layer // 2),
            _resident((1, GLA_HEADS * GLA_DK)),
            _resident((ROWS, ROWS)),
            _resident((1, D_MODEL)),
            _weight_spec(w_out, layer // 2),
            _resident((1, D_MODEL)),
            _resident((1, D_MODEL)),
            _weight_spec(w_up, layer),
            _weight_spec(w_down, layer),
            _resident((1, D_MODEL)),
        ] + side_in,
        out_specs=[out_spec] + side_out,
        scratch_shapes=[pltpu.VMEM((GLA_HEADS, GLA_DK, GLA_DV), F32)] + _layer_scratch(),
        compiler_params=pltpu.CompilerParams(
            dimension_semantics=("arbitrary",), vmem_limit_bytes=VMEM_LIMIT),
        name="odd_layer",
    )(h, g_pre, w_in, w_low, w_gu, g_bias, tri, ng, w_out, g_post, g_mlp, w_up, w_down, g_mlp_post,
      *[w for w, _, _ in side_sources])


def _retention_tables(seq):
    heads = jnp.arange(RET_HEADS, dtype=F32)
    log_gamma = jnp.log1p(-(2.0 ** (-5.0 - heads)))
    idx = jnp.arange(ROWS, dtype=F32)
    rel = idx[:, None] - idx[None, :]
    causal = rel >= 0
    dec = jnp.where(causal[None], jnp.exp(log_gamma[:, None, None] * jnp.where(causal, rel, 0.0)[None]), 0.0)
    k_dec = jnp.exp(log_gamma[None, :] * (ROWS - 1 - idx)[:, None])
    q_dec = jnp.exp(log_gamma[None, :] * (idx + 1.0)[:, None])
    c_dec = jnp.exp(log_gamma * ROWS)
    shape = (RET_HEADS, ROWS, RET_D)
    kdec = jnp.broadcast_to(k_dec.T[:, :, None], shape)
    qdec = jnp.broadcast_to(q_dec.T[:, :, None], shape)
    cdec = jnp.broadcast_to(c_dec[:, None, None], (RET_HEADS, RET_D, RET_D))
    half = RET_D // 2
    inv = ROPE_BASE ** (-np.arange(half, dtype=np.float64) / half)
    ang = np.arange(seq, dtype=np.float64)[:, None] * inv[None, :]
    cos, sin = np.cos(ang), np.sin(ang)
    cos2 = jnp.asarray(np.concatenate([cos, cos], axis=-1), F32)
    sin2 = jnp.asarray(np.concatenate([-sin, sin], axis=-1), F32)
    return dec, qdec, kdec, cdec, cos2, sin2


def kernel(x, norm_g, w_up, w_down, ev_w_in, ev_ret_norm_g, ev_sinks, ev_w_out, od_w_in,
           od_w_gate_up, od_gate_bias, od_norm_g, od_w_out):
    batch, seq, d = x.shape
    depth = norm_g.shape[0]
    h = x.reshape(batch * seq, d)
    dec, qdec, kdec, cdec, cos2, sin2 = _retention_tables(seq)
    pos = np.arange(ROWS)
    tri = jnp.asarray((pos[None, :] <= pos[:, None])
                      & (pos[None, :] // GLA_CHUNK == pos[:, None] // GLA_CHUNK), BF16)
    w_up_b, w_down_b = w_up[0].astype(BF16), w_down[0].astype(BF16)
    od_w_out_b = None
    od_w_in_b = od_w_in.astype(BF16)
    n_ev = ev_w_in.shape[0]
    aq_cols = ev_w_in[:, :, EV_AQ:EV_AK].reshape(n_ev, d, SWA_KV_HEADS, SWA_GROUP, SWA_HD)
    aq_cols = aq_cols.transpose(0, 1, 3, 2, 4).reshape(n_ev, d, EV_AK - EV_AQ)
    ev_w_in_b = jnp.concatenate(
        [ev_w_in[:, :, :EV_AQ], aq_cols, ev_w_in[:, :, EV_AK:]], axis=-1).astype(BF16)
    att_rows = ev_w_out[:, EV_RET_MIX:, :].reshape(n_ev, SWA_KV_HEADS, SWA_GROUP, SWA_HD, d)
    att_rows = att_rows.transpose(0, 2, 1, 3, 4).reshape(n_ev, SWA_HEADS * SWA_HD, d)
    ev_w_out_b = jnp.concatenate([ev_w_out[:, :EV_RET_MIX, :], att_rows], axis=1).astype(BF16)
    od_w_low_b = jnp.pad(od_w_in[:, :, OD_END:], ((0, 0), (0, 0), (0, RANK_PAD - GLA_RANK))).astype(BF16)
    od_w_gu_b = jnp.pad(od_w_gate_up, ((0, 0), (0, RANK_PAD - GLA_RANK), (0, 0))).astype(BF16)
    g = lambda layer, j: norm_g[layer, j].reshape(1, d)
    for layer in range(depth):
        i = layer // 2
        nxt = layer + 1
        side = []
        if nxt < depth:
            side = [(w_up, nxt, D_FF), (w_down, nxt, d)]
            if nxt % 2 == 1:
                side += [(od_w_out, nxt // 2, d)]
        mlp_args = (g(layer, 2), w_up_b, w_down_b, g(layer, 3), seq, layer, side)
        if layer % 2 == 0:
            h, *cast = _even_layer(h, ev_sinks[i], g(layer, 0), ev_w_in_b, cos2, sin2, dec, qdec, kdec,
                                   cdec, ev_ret_norm_g[i].reshape(1, -1), ev_w_out_b, g(layer, 1),
                                   *mlp_args)
        else:
            h, *cast = _odd_layer(h, g(layer, 0), od_w_in_b, od_w_low_b, od_w_gu_b,
                                  od_gate_bias[i].reshape(1, -1), tri, od_norm_g[i].reshape(1, -1),
                                  od_w_out_b, g(layer, 1), *mlp_args)
        if cast:
            w_up_b, w_down_b = cast[:2]
        if len(cast) == 3:
            od_w_out_b = cast[2]
    return h.reshape(batch, seq, d)
```

```python
import functools

import numpy as np
import jax
import jax.numpy as jnp
from jax import lax
from jax.experimental import pallas as pl
from jax.experimental.pallas import tpu as pltpu

F32 = jnp.float32
BF16 = jnp.bfloat16

D_MODEL = 1024
D_FF = 4 * D_MODEL
NORM_EPS = 1e-6
NEG_INF = -1e30

RET_HEADS = 4
RET_D = 128
ROPE_BASE = 10000.0

SWA_HEADS = 8
SWA_KV_HEADS = 2
SWA_GROUP = SWA_HEADS // SWA_KV_HEADS
SWA_HD = 64
WINDOW = 128

GLA_HEADS = 4
GLA_DK = 128
GLA_DV = 256
GLA_RANK = 16
GLA_TAU = 16.0
GLA_CHUNK = 64

LANES = 128
SUBLANES = 8
RANK_PAD = LANES

EV_RQ, EV_RK, EV_RV, EV_RG, EV_AQ, EV_AK, EV_AV, EV_END = 0, 512, 1024, 1536, 2048, 2560, 2688, 2816
EV_RET_MIX = RET_HEADS * RET_D
OD_Q, OD_K, OD_V, OD_R, OD_END = 0, 512, 1024, 2048, 3072

ROWS = 256
SUB_BLOCKS = 2
STEP_ROWS = SUB_BLOCKS * ROWS
FF_CHUNK = 1024
SWA_HEADS_PER_MLP_STEP = 4
EVEN_OUT_PIECES = 1
ODD_OUT_PIECES = 2
VMEM_LIMIT = 56 * 1024 * 1024

SWA_STACK_HEADS = tuple(h for p in range(SWA_GROUP) for h in (p, SWA_GROUP + p))


def _rms(x, g):
    return x * lax.rsqrt(jnp.mean(x * x, axis=-1, keepdims=True) + NORM_EPS) * g


def _silu(x):
    return x * (1.0 / (1.0 + jnp.exp(-x)))


def _dot(a, b):
    return jnp.dot(a.astype(BF16), b.astype(BF16), preferred_element_type=F32)


def _dot_nt(a, b):
    return lax.dot_general(a.astype(BF16), b.astype(BF16), (((1,), (1,)), ((), ())),
                           preferred_element_type=F32)


def _dot_tn(a, b):
    return lax.dot_general(a.astype(BF16), b.astype(BF16), (((0,), (0,)), ((), ())),
                           preferred_element_type=F32)


def _mlp_steps(x, gpre_ref, wup_ref, wdown_ref, gpost_ref, o_ref):
    u = _rms(x, gpre_ref[...]).astype(BF16)
    yield
    acc = jnp.zeros(x.shape, F32)
    n_chunks = D_FF // FF_CHUNK
    for c in range(n_chunks):
        cols = slice(c * FF_CHUNK, (c + 1) * FF_CHUNK)
        hid = jnp.maximum(jnp.dot(u, wup_ref[:, cols], preferred_element_type=F32), 0.0)
        hid = (hid * hid).astype(BF16)
        yield
        acc = acc + jnp.dot(hid, wdown_ref[cols, :], preferred_element_type=F32)
        if c == n_chunks - 1:
            o_ref[...] = x + _rms(acc, gpost_ref[...])
        yield


def _out_proj(x, mix_ref, wout_ref, gpost_ref, n_pieces):
    pieces = []
    for p in range(n_pieces):
        rows = slice(p * STEP_ROWS // n_pieces, (p + 1) * STEP_ROWS // n_pieces)
        out = jnp.dot(mix_ref[rows, :], wout_ref[...], preferred_element_type=F32)
        pieces.append(x[rows] + _rms(out, gpost_ref[...]))
    return jnp.concatenate(pieces, axis=0)


def _sub_block_rows():
    return [slice(sb * ROWS, (sb + 1) * ROWS) for sb in range(SUB_BLOCKS)]


def _retention_swa(mlp, first_in_seq, rq, rk, rv, rg, aq, ak, av, cos, sin, sinks_ref, dec_ref, qdec_ref,
                   kdec_ref, cdec_ref, rng_ref, state_ref, kprev_ref, vprev_ref, mix_ref):
    heads = range(RET_HEADS)
    head_cols = [slice(hd * RET_D, (hd + 1) * RET_D) for hd in heads]
    qs = [rq[:, c] * cos + pltpu.roll(rq[:, c], RET_D // 2, 1) * sin for c in head_cols]
    ks = [(rk[:, c] * cos + pltpu.roll(rk[:, c], RET_D // 2, 1) * sin) * (RET_D ** -0.5)
          for c in head_cols]
    vs = [rv[:, c].astype(BF16) for c in head_cols]
    states = [state_ref[hd] for hd in heads]
    scores = [_dot_nt(qs[hd], ks[hd]) * dec_ref[hd] for hd in heads]
    cross = [_dot(qs[hd] * qdec_ref[hd], states[hd]) for hd in heads]
    for hd in heads:
        state_ref[hd] = cdec_ref[hd] * states[hd] + _dot_tn(ks[hd] * kdec_ref[hd], vs[hd])
    outs = [_dot(scores[hd], vs[hd]) + cross[hd] for hd in heads]
    for hd in heads:
        o = outs[hd]
        y = o * lax.rsqrt(jnp.mean(o * o, axis=-1, keepdims=True) + NORM_EPS)
        mix_ref[:, head_cols[hd]] = (y * rng_ref[:, head_cols[hd]] * _silu(rg[:, head_cols[hd]])).astype(BF16)

    low = lax.broadcasted_iota(jnp.int32, (WINDOW, LANES), 1) < SWA_HD
    qi = lax.broadcasted_iota(jnp.int32, (WINDOW, 2 * WINDOW), 0)
    ki = lax.broadcasted_iota(jnp.int32, (WINDOW, 2 * WINDOW), 1)
    allowed = (ki > qi) & (ki <= qi + WINDOW)
    first_mask = allowed & (ki >= jnp.where(first_in_seq, WINDOW, 0))
    for c in range(ROWS // WINDOW):
        rows = slice(c * WINDOW, (c + 1) * WINDOW)
        if c == 0:
            kp, vp = kprev_ref[...], vprev_ref[...]
            mask = first_mask
        else:
            prev = slice((c - 1) * WINDOW, c * WINDOW)
            kp, vp = ak[prev], av[prev]
            mask = allowed
        kk = jnp.concatenate([kp, ak[rows]], axis=0)
        vv = jnp.concatenate([vp, av[rows]], axis=0)
        q_parts = []
        for p in range(SWA_GROUP):
            qg = aq[rows, p * LANES:(p + 1) * LANES]
            q_parts.append(jnp.where(low, qg, 0.0))
            q_parts.append(jnp.where(low, 0.0, qg))
        s_all = _dot_nt(jnp.concatenate(q_parts, axis=0), kk)
        ps = []
        for j, head in enumerate(SWA_STACK_HEADS):
            s = jnp.where(mask, s_all[j * WINDOW:(j + 1) * WINDOW], NEG_INF)
            sink = sinks_ref[head]
            m = jnp.maximum(jnp.max(s, axis=-1, keepdims=True), sink)
            p_un = jnp.exp(s - m)
            denom = jnp.sum(p_un, axis=-1, keepdims=True) + jnp.exp(sink - m)
            ps.append((p_un * (1.0 / denom)).astype(BF16))
            if j % SWA_HEADS_PER_MLP_STEP == SWA_HEADS_PER_MLP_STEP - 1:
                next(mlp, None)
        o_all = _dot(jnp.concatenate(ps, axis=0), vv)
        for p in range(SWA_GROUP):
            o_lo = o_all[(2 * p) * WINDOW:(2 * p + 1) * WINDOW]
            o_hi = o_all[(2 * p + 1) * WINDOW:(2 * p + 2) * WINDOW]
            mix_ref[rows, EV_RET_MIX + p * LANES:EV_RET_MIX + (p + 1) * LANES] = (
                jnp.where(low, o_lo, o_hi).astype(BF16))
    kprev_ref[...] = ak[ROWS - WINDOW:]
    vprev_ref[...] = av[ROWS - WINDOW:]


EVEN_INPUTS = 18
ODD_INPUTS = 14


def _split_refs(refs, n_inputs, n_side):
    a, b, c = n_inputs, n_inputs + n_side, n_inputs + n_side + 1
    return refs[:a], refs[a:b], refs[b], refs[c:c + n_side], refs[c + n_side:]


def _even_kernel(steps_per_seq, n_side, *refs):
    inputs, side_src, o_ref, side_dst, scratch = _split_refs(refs, EVEN_INPUTS, n_side)
    (sinks_ref, h_ref, gpre_ref, win_ref, waq_ref, cos_ref, sin_ref, dec_ref, qdec_ref, kdec_ref, cdec_ref,
     rng_ref, wout_ref, gpost_ref, gmlp_ref, wup_ref, wdown_ref, gmlp_post_ref) = inputs
    state_ref, kprev_ref, vprev_ref, mix_ref, h1_ref = scratch
    _side_cast(side_src, side_dst)
    step = pl.program_id(0)
    first_step_of_seq = step % steps_per_seq == 0

    @pl.when(step == 0)
    def _():
        h1_ref[...] = jnp.zeros_like(h1_ref)

    @pl.when(first_step_of_seq)
    def _():
        state_ref[...] = jnp.zeros_like(state_ref)
        kprev_ref[...] = jnp.zeros_like(kprev_ref)
        vprev_ref[...] = jnp.zeros_like(vprev_ref)

    mlp = _mlp_steps(h1_ref[(step + 1) % 2], gmlp_ref, wup_ref, wdown_ref, gmlp_post_ref, o_ref)
    x = h_ref[...]
    u = _rms(x, gpre_ref[...]).astype(BF16)
    next(mlp, None)

    def proj(a, b):
        return jnp.dot(u, win_ref[:, a:b], preferred_element_type=F32)

    rq = proj(EV_RQ, EV_RK)
    rk = proj(EV_RK, EV_RV)
    rv = proj(EV_RV, EV_RG)
    rg = proj(EV_RG, EV_AQ)
    aq = jnp.dot(u, waq_ref[...], preferred_element_type=F32) * (SWA_HD ** -0.5)
    ak = proj(EV_AK, EV_AV)
    av = proj(EV_AV, EV_END)
    for sb, rows in enumerate(_sub_block_rows()):
        _retention_swa(
            mlp, first_step_of_seq if sb == 0 else False, rq[rows], rk[rows], rv[rows], rg[rows],
            aq[rows], ak[rows], av[rows], cos_ref[rows, :], sin_ref[rows, :], sinks_ref, dec_ref,
            qdec_ref, kdec_ref, cdec_ref, rng_ref, state_ref, kprev_ref, vprev_ref, mix_ref.at[rows])
    for _ in mlp:
        pass
    h1_ref[step % 2] = _out_proj(x, mix_ref, wout_ref, gpost_ref, EVEN_OUT_PIECES)


def _resident(shape):
    zeros = (0,) * len(shape)
    return pl.BlockSpec(shape, lambda s: zeros, pipeline_mode=pl.Buffered(1))


def _stacked(shape, index):
    zeros = (0,) * len(shape)
    return pl.BlockSpec((None,) + tuple(shape), lambda s: (index,) + zeros, pipeline_mode=pl.Buffered(1))


def _weight_spec(w, index):
    if w.ndim == 2:
        return _resident(w.shape)
    return _stacked(w.shape[1:], index)


def _side_cast_specs(n_blocks, sources):
    in_specs, out_specs, out_shapes = [], [], []
    for w, index, cols in sources:
        slab_rows = w.shape[1] // n_blocks
        in_specs.append(pl.BlockSpec(
            (None, slab_rows, w.shape[2]),
            lambda s, index=index: (index, jnp.minimum(s, n_blocks - 1), 0)))
        out_specs.append(pl.BlockSpec((slab_rows, cols), lambda s: (jnp.minimum(s, n_blocks - 1), 0)))
        out_shapes.append(jax.ShapeDtypeStruct((w.shape[1], cols), BF16))
    return in_specs, out_specs, out_shapes


def _side_cast(src_refs, dst_refs):
    for src, dst in zip(src_refs, dst_refs):
        dst[...] = src[:, :dst.shape[1]].astype(BF16)


def _row_specs(n_blocks):
    in_spec = pl.BlockSpec((STEP_ROWS, D_MODEL), lambda s: (jnp.minimum(s, n_blocks - 1), 0))
    out_spec = pl.BlockSpec((STEP_ROWS, D_MODEL), lambda s: (jnp.maximum(s - 1, 0), 0))
    return in_spec, out_spec


def _layer_scratch():
    return [pltpu.VMEM((STEP_ROWS, D_MODEL), BF16),
            pltpu.VMEM((2, STEP_ROWS, D_MODEL), F32)]


def _even_layer(h, sinks, g_pre, w_in, w_aq, cos2, sin2, dec, qdec, kdec, cdec, ret_g, w_out, g_post,
                g_mlp, w_up, w_down, g_mlp_post, seq, layer, side_sources):
    nt = seq // STEP_ROWS
    n_blocks = h.shape[0] // STEP_ROWS
    in_spec, out_spec = _row_specs(n_blocks)
    pos_spec = pl.BlockSpec((STEP_ROWS, LANES), lambda s: (jnp.minimum(s, n_blocks - 1) % nt, 0))
    side_in, side_out, side_shapes = _side_cast_specs(n_blocks, side_sources)
    return pl.pallas_call(
        functools.partial(_even_kernel, nt, len(side_sources)),
        out_shape=[jax.ShapeDtypeStruct(h.shape, h.dtype)] + side_shapes,
        grid=(n_blocks + 1,),
        in_specs=[
            pl.BlockSpec(memory_space=pltpu.SMEM),
            in_spec,
            _resident((1, D_MODEL)),
            _weight_spec(w_in, layer // 2),
            _weight_spec(w_aq, layer // 2),
            pos_spec, pos_spec,
            _resident((RET_HEADS, ROWS, ROWS)),
            _resident((RET_HEADS, ROWS, RET_D)),
            _resident((RET_HEADS, ROWS, RET_D)),
            _resident((RET_HEADS, RET_D, RET_D)),
            _resident((1, RET_HEADS * RET_D)),
            _weight_spec(w_out, layer // 2),
            _resident((1, D_MODEL)),
            _resident((1, D_MODEL)),
            _weight_spec(w_up, layer),
            _weight_spec(w_down, layer),
            _resident((1, D_MODEL)),
        ] + side_in,
        out_specs=[out_spec] + side_out,
        scratch_shapes=[
            pltpu.VMEM((RET_HEADS, RET_D, RET_D), F32),
            pltpu.VMEM((WINDOW, LANES), F32),
            pltpu.VMEM((WINDOW, LANES), F32),
        ] + _layer_scratch(),
        compiler_params=pltpu.CompilerParams(
            dimension_semantics=("arbitrary",), vmem_limit_bytes=VMEM_LIMIT),
        name="even_layer",
    )(sinks, h, g_pre, w_in, w_aq, cos2, sin2, dec, qdec, kdec, cdec, ret_g, w_out, g_post,
      g_mlp, w_up, w_down, g_mlp_post, *[w for w, _, _ in side_sources])


def _gla(mlp, q, k, v, r, lg_hi, lg_lo, tri_ref, ng_ref, state_ref, mix_ref):
    n_chunks = ROWS // GLA_CHUNK
    chunk_rows = [slice(c * GLA_CHUNK, (c + 1) * GLA_CHUNK) for c in range(n_chunks)]
    tri = tri_ref[...]
    b = (jnp.dot(tri, lg_hi, preferred_element_type=F32)
         + jnp.dot(tri, lg_lo, preferred_element_type=F32))
    next(mlp, None)
    b_last = [b[rows.stop - 1:rows.stop, :] for rows in chunk_rows]
    b_last_full = jnp.concatenate(
        [jnp.broadcast_to(bl, (GLA_CHUNK, bl.shape[1])) for bl in b_last], axis=0)
    q_in = q * jnp.exp(b)
    k_in = k * jnp.exp(-b)
    k_st = k * jnp.exp(b_last_full - b)

    ri = lax.broadcasted_iota(jnp.int32, (ROWS, ROWS), 0)
    ci = lax.broadcasted_iota(jnp.int32, (ROWS, ROWS), 1)
    causal = (ci <= ri) & (ci >= ri - (ri & (GLA_CHUNK - 1)))
    row_id = lax.broadcasted_iota(jnp.int32, (SUBLANES, GLA_DK), 0)
    heads = range(GLA_HEADS)
    kcs = [slice(hd * GLA_DK, (hd + 1) * GLA_DK) for hd in heads]
    vcs = [slice(hd * GLA_DV, (hd + 1) * GLA_DV) for hd in heads]
    scores = [jnp.where(causal, _dot_nt(q_in[:, kcs[hd]], k_in[:, kcs[hd]]), 0.0) for hd in heads]
    kvs = [[_dot_tn(k_st[rows, kcs[hd]], v[rows, vcs[hd]]) for rows in chunk_rows] for hd in heads]
    next(mlp, None)
    crosses = []
    for hd in heads:
        tile = jnp.zeros((SUBLANES, GLA_DK), F32)
        for c in range(n_chunks):
            tile = jnp.where(row_id == c, b_last[c][:, kcs[hd]], tile)
        tile = jnp.concatenate([tile, jnp.zeros((GLA_DK - SUBLANES, GLA_DK), F32)], axis=0).T
        state = state_ref[hd]
        cross = []
        for c in range(n_chunks):
            cross.append(_dot(q_in[chunk_rows[c], kcs[hd]], state))
            state = jnp.exp(tile[:, c:c + 1]) * state + kvs[hd][c]
        state_ref[hd] = state
        crosses.append(jnp.concatenate(cross, axis=0))
    next(mlp, None)
    outs = [_dot(scores[hd], v[:, vcs[hd]]) + crosses[hd] for hd in heads]
    next(mlp, None)
    for hd in heads:
        o = outs[hd]
        y = o * lax.rsqrt(jnp.mean(o * o, axis=-1, keepdims=True) + NORM_EPS)
        mix_ref[:, vcs[hd]] = (y * ng_ref[:, vcs[hd]] * _silu(r[:, vcs[hd]])).astype(BF16)


def _odd_kernel(steps_per_seq, n_side, *refs):
    inputs, side_src, o_ref, side_dst, scratch = _split_refs(refs, ODD_INPUTS, n_side)
    (h_ref, gpre_ref, win_ref, wlow_ref, wgu_ref, gbias_ref, tri_ref, ng_ref, wout_ref, gpost_ref,
     gmlp_ref, wup_ref, wdown_ref, gmlp_post_ref) = inputs
    state_ref, mix_ref, h1_ref = scratch
    _side_cast(side_src, side_dst)
    step = pl.program_id(0)

    @pl.when(step == 0)
    def _():
        h1_ref[...] = jnp.zeros_like(h1_ref)

    @pl.when(step % steps_per_seq == 0)
    def _():
        state_ref[...] = jnp.zeros_like(state_ref)

    mlp = _mlp_steps(h1_ref[(step + 1) % 2], gmlp_ref, wup_ref, wdown_ref, gmlp_post_ref, o_ref)
    x = h_ref[...]
    u = _rms(x, gpre_ref[...]).astype(BF16)
    next(mlp, None)

    def proj(a, b):
        return jnp.dot(u, win_ref[:, a:b], preferred_element_type=F32)

    a_low = jnp.dot(u, wlow_ref[...], preferred_element_type=F32)
    k = proj(OD_K, OD_V)
    gl = _dot(a_low, wgu_ref[...]) + gbias_ref[...]
    q = proj(OD_Q, OD_K) * (GLA_DK ** -0.5)
    log_g = (jnp.minimum(gl, 0.0) - jnp.log1p(jnp.exp(-jnp.abs(gl)))) * (1.0 / GLA_TAU)
    lg_hi = log_g.astype(BF16)
    lg_lo = (log_g - lg_hi.astype(F32)).astype(BF16)
    v = proj(OD_V, OD_R).astype(BF16)
    r = proj(OD_R, OD_END)
    for rows in _sub_block_rows():
        _gla(mlp, q[rows], k[rows], v[rows], r[rows], lg_hi[rows], lg_lo[rows], tri_ref, ng_ref,
             state_ref, mix_ref.at[rows])
    for _ in mlp:
        pass
    h1_ref[step % 2] = _out_proj(x, mix_ref, wout_ref, gpost_ref, ODD_OUT_PIECES)


def _odd_layer(h, g_pre, w_in, w_low, w_gu, g_bias, tri, ng, w_out, g_post,
               g_mlp, w_up, w_down, g_mlp_post, seq, layer, side_sources):
    nt = seq // STEP_ROWS
    n_blocks = h.shape[0] // STEP_ROWS
    in_spec, out_spec = _row_specs(n_blocks)
    side_in, side_out, side_shapes = _side_cast_specs(n_blocks, side_sources)
    return pl.pallas_call(
        functools.partial(_odd_kernel, nt, len(side_sources)),
        out_shape=[jax.ShapeDtypeStruct(h.shape, h.dtype)] + side_shapes,
        grid=(n_blocks + 1,),
        in_specs=[
            in_spec,
            _resident((1, D_MODEL)),
            _weight_spec(w_in, layer // 2),
            _stacked((D_MODEL, RANK_PAD), layer // 2),
            _stacked((RANK_PAD, GLA_HEADS * GLA_DK), layer // 2),
            _resident((1, GLA_HEADS * GLA_DK)),
            _resident((ROWS, ROWS)),
            _resident((1, D_MODEL)),
            _weight_spec(w_out, layer // 2),
            _resident((1, D_MODEL)),
            _resident((1, D_MODEL)),
            _weight_spec(w_up, layer),
            _weight_spec(w_down, layer),
            _resident((1, D_MODEL)),
        ] + side_in,
        out_specs=[out_spec] + side_out,
        scratch_shapes=[pltpu.VMEM((GLA_HEADS, GLA_DK, GLA_DV), F32)] + _layer_scratch(),
        compiler_params=pltpu.CompilerParams(
            dimension_semantics=("arbitrary",), vmem_limit_bytes=VMEM_LIMIT),
        name="odd_layer",
    )(h, g_pre, w_in, w_low, w_gu, g_bias, tri, ng, w_out, g_post, g_mlp, w_up, w_down, g_mlp_post,
      *[w for w, _, _ in side_sources])


def _retention_tables(seq):
    heads = jnp.arange(RET_HEADS, dtype=F32)
    log_gamma = jnp.log1p(-(2.0 ** (-5.0 - heads)))
    idx = jnp.arange(ROWS, dtype=F32)
    rel = idx[:, None] - idx[None, :]
    causal = rel >= 0
    dec = jnp.where(causal[None], jnp.exp(log_gamma[:, None, None] * jnp.where(causal, rel, 0.0)[None]), 0.0)
    k_dec = jnp.exp(log_gamma[None, :] * (ROWS - 1 - idx)[:, None])
    q_dec = jnp.exp(log_gamma[None, :] * (idx + 1.0)[:, None])
    c_dec = jnp.exp(log_gamma * ROWS)
    shape = (RET_HEADS, ROWS, RET_D)
    kdec = jnp.broadcast_to(k_dec.T[:, :, None], shape)
    qdec = jnp.broadcast_to(q_dec.T[:, :, None], shape)
    cdec = jnp.broadcast_to(c_dec[:, None, None], (RET_HEADS, RET_D, RET_D))
    half = RET_D // 2
    inv = ROPE_BASE ** (-np.arange(half, dtype=np.float64) / half)
    ang = np.arange(seq, dtype=np.float64)[:, None] * inv[None, :]
    cos, sin = np.cos(ang), np.sin(ang)
    cos2 = jnp.asarray(np.concatenate([cos, cos], axis=-1), F32)
    sin2 = jnp.asarray(np.concatenate([-sin, sin], axis=-1), F32)
    return dec, qdec, kdec, cdec, cos2, sin2


def kernel(x, norm_g, w_up, w_down, ev_w_in, ev_ret_norm_g, ev_sinks, ev_w_out, od_w_in,
           od_w_gate_up, od_gate_bias, od_norm_g, od_w_out):
    batch, seq, d = x.shape
    depth = norm_g.shape[0]
    h = x.reshape(batch * seq, d)
    dec, qdec, kdec, cdec, cos2, sin2 = _retention_tables(seq)
    pos = np.arange(ROWS)
    tri = jnp.asarray((pos[None, :] <= pos[:, None])
                      & (pos[None, :] // GLA_CHUNK == pos[:, None] // GLA_CHUNK), BF16)
    w_up_b, w_down_b = w_up[0].astype(BF16), w_down[0].astype(BF16)
    od_w_out_b = None
    od_w_in_b = od_w_in.astype(BF16)
    n_ev = ev_w_in.shape[0]
    aq_cols = ev_w_in[:, :, EV_AQ:EV_AK].reshape(n_ev, d, SWA_KV_HEADS, SWA_GROUP, SWA_HD)
    aq_cols = aq_cols.transpose(0, 1, 3, 2, 4).reshape(n_ev, d, EV_AK - EV_AQ)
    ev_w_in_b, ev_w_aq_b = ev_w_in.astype(BF16), aq_cols.astype(BF16)
    att_rows = ev_w_out[:, EV_RET_MIX:, :].reshape(n_ev, SWA_KV_HEADS, SWA_GROUP, SWA_HD, d)
    att_rows = att_rows.transpose(0, 2, 1, 3, 4).reshape(n_ev, SWA_HEADS * SWA_HD, d)
    ev_w_out_b = jnp.concatenate([ev_w_out[:, :EV_RET_MIX, :], att_rows], axis=1).astype(BF16)
    od_w_low_b = jnp.pad(od_w_in[:, :, OD_END:], ((0, 0), (0, 0), (0, RANK_PAD - GLA_RANK))).astype(BF16)
    od_w_gu_b = jnp.pad(od_w_gate_up, ((0, 0), (0, RANK_PAD - GLA_RANK), (0, 0))).astype(BF16)
    g = lambda layer, j: norm_g[layer, j].reshape(1, d)
    for layer in range(depth):
        i = layer // 2
        nxt = layer + 1
        side = []
        if nxt < depth:
            side = [(w_up, nxt, D_FF), (w_down, nxt, d)]
            if nxt % 2 == 1:
                side += [(od_w_out, nxt // 2, d)]
        mlp_args = (g(layer, 2), w_up_b, w_down_b, g(layer, 3), seq, layer, side)
        if layer % 2 == 0:
            h, *cast = _even_layer(h, ev_sinks[i], g(layer, 0), ev_w_in_b, ev_w_aq_b, cos2, sin2, dec, qdec,
                                   kdec, cdec, ev_ret_norm_g[i].reshape(1, -1), ev_w_out_b, g(layer, 1),
                                   *mlp_args)
        else:
            h, *cast = _odd_layer(h, g(layer, 0), od_w_in_b, od_w_low_b, od_w_gu_b,
                                  od_gate_bias[i].reshape(1, -1), tri, od_norm_g[i].reshape(1, -1),
                                  od_w_out_b, g(layer, 1), *mlp_args)
        if cast:
            w_up_b, w_down_b = cast[:2]
        if len(cast) == 3:
            od_w_out_b = cast[2]
    return h.reshape(batch, seq, d)
```

```python
import functools

import numpy as np
import jax
import jax.numpy as jnp
from jax import lax
from jax.experimental import pallas as pl
from jax.experimental.pallas import tpu as pltpu

F32 = jnp.float32
BF16 = jnp.bfloat16

D_MODEL = 1024
D_FF = 4 * D_MODEL
NORM_EPS = 1e-6
NEG_INF = -1e30

RET_HEADS = 4
RET_D = 128
ROPE_BASE = 10000.0

SWA_HEADS = 8
SWA_KV_HEADS = 2
SWA_GROUP = SWA_HEADS // SWA_KV_HEADS
SWA_HD = 64
WINDOW = 128

GLA_HEADS = 4
GLA_DK = 128
GLA_DV = 256
GLA_RANK = 16
GLA_TAU = 16.0
GLA_CHUNK = 64

LANES = 128
SUBLANES = 8
RANK_PAD = LANES

EV_RQ, EV_RK, EV_RV, EV_RG, EV_AQ, EV_AK, EV_AV, EV_END = 0, 512, 1024, 1536, 2048, 2560, 2688, 2816
EV_RET_MIX = RET_HEADS * RET_D
OD_Q, OD_K, OD_V, OD_R, OD_END = 0, 512, 1024, 2048, 3072

ROWS = 256
SUB_BLOCKS = 2
STEP_ROWS = SUB_BLOCKS * ROWS
FF_CHUNK = 1024
SWA_HEADS_PER_MLP_STEP = 4
EVEN_OUT_PIECES = 1
ODD_OUT_PIECES = 2
VMEM_LIMIT = 56 * 1024 * 1024

SWA_STACK_HEADS = tuple(h for p in range(SWA_GROUP) for h in (p, SWA_GROUP + p))


def _rms(x, g):
    return x * lax.rsqrt(jnp.mean(x * x, axis=-1, keepdims=True) + NORM_EPS) * g


def _silu(x):
    return x * (1.0 / (1.0 + jnp.exp(-x)))


def _dot(a, b):
    return jnp.dot(a.astype(BF16), b.astype(BF16), preferred_element_type=F32)


def _dot_nt(a, b):
    return lax.dot_general(a.astype(BF16), b.astype(BF16), (((1,), (1,)), ((), ())),
                           preferred_element_type=F32)


def _dot_tn(a, b):
    return lax.dot_general(a.astype(BF16), b.astype(BF16), (((0,), (0,)), ((), ())),
                           preferred_element_type=F32)


def _mlp_steps(x, gpre_ref, wup_ref, wdown_ref, gpost_ref, o_ref):
    u = _rms(x, gpre_ref[...]).astype(BF16)
    yield
    acc = jnp.zeros(x.shape, F32)
    n_chunks = D_FF // FF_CHUNK
    for c in range(n_chunks):
        cols = slice(c * FF_CHUNK, (c + 1) * FF_CHUNK)
        hid = jnp.maximum(jnp.dot(u, wup_ref[:, cols], preferred_element_type=F32), 0.0)
        hid = (hid * hid).astype(BF16)
        yield
        acc = acc + jnp.dot(hid, wdown_ref[cols, :], preferred_element_type=F32)
        if c == n_chunks - 1:
            o_ref[...] = x + _rms(acc, gpost_ref[...])
        yield


def _out_proj(x, mix_ref, wout_ref, gpost_ref, n_pieces):
    pieces = []
    for p in range(n_pieces):
        rows = slice(p * STEP_ROWS // n_pieces, (p + 1) * STEP_ROWS // n_pieces)
        out = jnp.dot(mix_ref[rows, :], wout_ref[...], preferred_element_type=F32)
        pieces.append(x[rows] + _rms(out, gpost_ref[...]))
    return jnp.concatenate(pieces, axis=0)


def _sub_block_rows():
    return [slice(sb * ROWS, (sb + 1) * ROWS) for sb in range(SUB_BLOCKS)]


def _retention_swa(mlp, first_in_seq, rq, rk, rv, rg, aq, ak, av, cos, sin, sinks_ref, dec_ref, qdec_ref,
                   kdec_ref, cdec_ref, rng_ref, state_ref, kprev_ref, vprev_ref, mix_ref):
    heads = range(RET_HEADS)
    head_cols = [slice(hd * RET_D, (hd + 1) * RET_D) for hd in heads]
    qs = [rq[:, c] * cos + pltpu.roll(rq[:, c], RET_D // 2, 1) * sin for c in head_cols]
    ks = [(rk[:, c] * cos + pltpu.roll(rk[:, c], RET_D // 2, 1) * sin) * (RET_D ** -0.5)
          for c in head_cols]
    vs = [rv[:, c].astype(BF16) for c in head_cols]
    states = [state_ref[hd] for hd in heads]
    scores = [_dot_nt(qs[hd], ks[hd]) * dec_ref[hd] for hd in heads]
    cross = [_dot(qs[hd] * qdec_ref[hd], states[hd]) for hd in heads]
    for hd in heads:
        state_ref[hd] = cdec_ref[hd] * states[hd] + _dot_tn(ks[hd] * kdec_ref[hd], vs[hd])
    outs = [_dot(scores[hd], vs[hd]) + cross[hd] for hd in heads]
    for hd in heads:
        o = outs[hd]
        y = o * lax.rsqrt(jnp.mean(o * o, axis=-1, keepdims=True) + NORM_EPS)
        mix_ref[:, head_cols[hd]] = (y * rng_ref[:, head_cols[hd]] * _silu(rg[:, head_cols[hd]])).astype(BF16)

    low = lax.broadcasted_iota(jnp.int32, (WINDOW, LANES), 1) < SWA_HD
    qi = lax.broadcasted_iota(jnp.int32, (WINDOW, 2 * WINDOW), 0)
    ki = lax.broadcasted_iota(jnp.int32, (WINDOW, 2 * WINDOW), 1)
    allowed = (ki > qi) & (ki <= qi + WINDOW)
    first_mask = allowed & (ki >= jnp.where(first_in_seq, WINDOW, 0))
    for c in range(ROWS // WINDOW):
        rows = slice(c * WINDOW, (c + 1) * WINDOW)
        if c == 0:
            kp, vp = kprev_ref[...], vprev_ref[...]
            mask = first_mask
        else:
            prev = slice((c - 1) * WINDOW, c * WINDOW)
            kp, vp = ak[prev], av[prev]
            mask = allowed
        kk = jnp.concatenate([kp, ak[rows]], axis=0)
        vv = jnp.concatenate([vp, av[rows]], axis=0)
        q_parts = []
        for p in range(SWA_GROUP):
            qg = aq[rows, p * LANES:(p + 1) * LANES]
            q_parts.append(jnp.where(low, qg, 0.0))
            q_parts.append(jnp.where(low, 0.0, qg))
        s_all = _dot_nt(jnp.concatenate(q_parts, axis=0), kk)
        ps = []
        for j, head in enumerate(SWA_STACK_HEADS):
            s = jnp.where(mask, s_all[j * WINDOW:(j + 1) * WINDOW], NEG_INF)
            sink = sinks_ref[head]
            m = jnp.maximum(jnp.max(s, axis=-1, keepdims=True), sink)
            p_un = jnp.exp(s - m)
            denom = jnp.sum(p_un, axis=-1, keepdims=True) + jnp.exp(sink - m)
            ps.append((p_un * (1.0 / denom)).astype(BF16))
            if j % SWA_HEADS_PER_MLP_STEP == SWA_HEADS_PER_MLP_STEP - 1:
                next(mlp, None)
        o_all = _dot(jnp.concatenate(ps, axis=0), vv)
        for p in range(SWA_GROUP):
            o_lo = o_all[(2 * p) * WINDOW:(2 * p + 1) * WINDOW]
            o_hi = o_all[(2 * p + 1) * WINDOW:(2 * p + 2) * WINDOW]
            mix_ref[rows, EV_RET_MIX + p * LANES:EV_RET_MIX + (p + 1) * LANES] = (
                jnp.where(low, o_lo, o_hi).astype(BF16))
    kprev_ref[...] = ak[ROWS - WINDOW:]
    vprev_ref[...] = av[ROWS - WINDOW:]


EVEN_INPUTS = 18
ODD_INPUTS = 14


def _split_refs(refs, n_inputs, n_side):
    a, b, c = n_inputs, n_inputs + n_side, n_inputs + n_side + 1
    return refs[:a], refs[a:b], refs[b], refs[c:c + n_side], refs[c + n_side:]


def _even_kernel(steps_per_seq, n_side, *refs):
    inputs, side_src, o_ref, side_dst, scratch = _split_refs(refs, EVEN_INPUTS, n_side)
    (sinks_ref, h_ref, gpre_ref, win_ref, waq_ref, cos_ref, sin_ref, dec_ref, qdec_ref, kdec_ref, cdec_ref,
     rng_ref, wout_ref, gpost_ref, gmlp_ref, wup_ref, wdown_ref, gmlp_post_ref) = inputs
    state_ref, kprev_ref, vprev_ref, mix_ref, h1_ref = scratch
    _side_cast(side_src, side_dst)
    step = pl.program_id(0)
    first_step_of_seq = step % steps_per_seq == 0

    @pl.when(first_step_of_seq)
    def _():
        state_ref[...] = jnp.zeros_like(state_ref)
        kprev_ref[...] = jnp.zeros_like(kprev_ref)
        vprev_ref[...] = jnp.zeros_like(vprev_ref)

    run = functools.partial(_even_step, step, first_step_of_seq, inputs, o_ref, scratch)
    pl.when(step == 0)(lambda: run(False))
    pl.when(step > 0)(lambda: run(True))


def _even_step(step, first_step_of_seq, inputs, o_ref, scratch, run_mlp):
    (sinks_ref, h_ref, gpre_ref, win_ref, waq_ref, cos_ref, sin_ref, dec_ref, qdec_ref, kdec_ref, cdec_ref,
     rng_ref, wout_ref, gpost_ref, gmlp_ref, wup_ref, wdown_ref, gmlp_post_ref) = inputs
    state_ref, kprev_ref, vprev_ref, mix_ref, h1_ref = scratch
    mlp = iter(())
    if run_mlp:
        mlp = _mlp_steps(h1_ref[(step + 1) % 2], gmlp_ref, wup_ref, wdown_ref, gmlp_post_ref, o_ref)
    x = h_ref[...]
    u = _rms(x, gpre_ref[...]).astype(BF16)
    next(mlp, None)

    def proj(a, b):
        return jnp.dot(u, win_ref[:, a:b], preferred_element_type=F32)

    rq = proj(EV_RQ, EV_RK)
    rk = proj(EV_RK, EV_RV)
    rv = proj(EV_RV, EV_RG)
    rg = proj(EV_RG, EV_AQ)
    aq = jnp.dot(u, waq_ref[...], preferred_element_type=F32) * (SWA_HD ** -0.5)
    ak = proj(EV_AK, EV_AV)
    av = proj(EV_AV, EV_END)
    for sb, rows in enumerate(_sub_block_rows()):
        _retention_swa(
            mlp, first_step_of_seq if sb == 0 else False, rq[rows], rk[rows], rv[rows], rg[rows],
            aq[rows], ak[rows], av[rows], cos_ref[rows, :], sin_ref[rows, :], sinks_ref, dec_ref,
            qdec_ref, kdec_ref, cdec_ref, rng_ref, state_ref, kprev_ref, vprev_ref, mix_ref.at[rows])
    for _ in mlp:
        pass
    h1_ref[step % 2] = _out_proj(x, mix_ref, wout_ref, gpost_ref, EVEN_OUT_PIECES)


def _resident(shape):
    zeros = (0,) * len(shape)
    return pl.BlockSpec(shape, lambda s: zeros, pipeline_mode=pl.Buffered(1))


def _stacked(shape, index):
    zeros = (0,) * len(shape)
    return pl.BlockSpec((None,) + tuple(shape), lambda s: (index,) + zeros, pipeline_mode=pl.Buffered(1))


def _weight_spec(w, index):
    if w.ndim == 2:
        return _resident(w.shape)
    return _stacked(w.shape[1:], index)


def _side_cast_specs(n_blocks, sources):
    in_specs, out_specs, out_shapes = [], [], []
    for w, index, cols in sources:
        slab_rows = w.shape[1] // n_blocks
        in_specs.append(pl.BlockSpec(
            (None, slab_rows, w.shape[2]),
            lambda s, index=index: (index, jnp.minimum(s, n_blocks - 1), 0)))
        out_specs.append(pl.BlockSpec((slab_rows, cols), lambda s: (jnp.minimum(s, n_blocks - 1), 0)))
        out_shapes.append(jax.ShapeDtypeStruct((w.shape[1], cols), BF16))
    return in_specs, out_specs, out_shapes


def _side_cast(src_refs, dst_refs):
    for src, dst in zip(src_refs, dst_refs):
        dst[...] = src[:, :dst.shape[1]].astype(BF16)


def _row_specs(n_blocks):
    in_spec = pl.BlockSpec((STEP_ROWS, D_MODEL), lambda s: (jnp.minimum(s, n_blocks - 1), 0))
    out_spec = pl.BlockSpec((STEP_ROWS, D_MODEL), lambda s: (jnp.maximum(s - 1, 0), 0))
    return in_spec, out_spec


def _layer_scratch():
    return [pltpu.VMEM((STEP_ROWS, D_MODEL), BF16),
            pltpu.VMEM((2, STEP_ROWS, D_MODEL), F32)]


def _even_layer(h, sinks, g_pre, w_in, w_aq, cos2, sin2, dec, qdec, kdec, cdec, ret_g, w_out, g_post,
                g_mlp, w_up, w_down, g_mlp_post, seq, layer, side_sources):
    nt = seq // STEP_ROWS
    n_blocks = h.shape[0] // STEP_ROWS
    in_spec, out_spec = _row_specs(n_blocks)
    pos_spec = pl.BlockSpec((STEP_ROWS, LANES), lambda s: (jnp.minimum(s, n_blocks - 1) % nt, 0))
    side_in, side_out, side_shapes = _side_cast_specs(n_blocks, side_sources)
    return pl.pallas_call(
        functools.partial(_even_kernel, nt, len(side_sources)),
        out_shape=[jax.ShapeDtypeStruct(h.shape, h.dtype)] + side_shapes,
        grid=(n_blocks + 1,),
        in_specs=[
            pl.BlockSpec(memory_space=pltpu.SMEM),
            in_spec,
            _resident((1, D_MODEL)),
            _weight_spec(w_in, layer // 2),
            _weight_spec(w_aq, layer // 2),
            pos_spec, pos_spec,
            _resident((RET_HEADS, ROWS, ROWS)),
            _resident((RET_HEADS, ROWS, RET_D)),
            _resident((RET_HEADS, ROWS, RET_D)),
            _resident((RET_HEADS, RET_D, RET_D)),
            _resident((1, RET_HEADS * RET_D)),
            _weight_spec(w_out, layer // 2),
            _resident((1, D_MODEL)),
            _resident((1, D_MODEL)),
            _weight_spec(w_up, layer),
            _weight_spec(w_down, layer),
            _resident((1, D_MODEL)),
        ] + side_in,
        out_specs=[out_spec] + side_out,
        scratch_shapes=[
            pltpu.VMEM((RET_HEADS, RET_D, RET_D), F32),
            pltpu.VMEM((WINDOW, LANES), F32),
            pltpu.VMEM((WINDOW, LANES), F32),
        ] + _layer_scratch(),
        compiler_params=pltpu.CompilerParams(
            dimension_semantics=("arbitrary",), vmem_limit_bytes=VMEM_LIMIT),
        name="even_layer",
    )(sinks, h, g_pre, w_in, w_aq, cos2, sin2, dec, qdec, kdec, cdec, ret_g, w_out, g_post,
      g_mlp, w_up, w_down, g_mlp_post, *[w for w, _, _ in side_sources])


def _gla(mlp, q, k, v, r, lg_hi, lg_lo, tri_ref, ng_ref, state_ref, mix_ref):
    n_chunks = ROWS // GLA_CHUNK
    chunk_rows = [slice(c * GLA_CHUNK, (c + 1) * GLA_CHUNK) for c in range(n_chunks)]
    tri = tri_ref[...]
    b = (jnp.dot(tri, lg_hi, preferred_element_type=F32)
         + jnp.dot(tri, lg_lo, preferred_element_type=F32))
    next(mlp, None)
    b_last = [b[rows.stop - 1:rows.stop, :] for rows in chunk_rows]
    b_last_full = jnp.concatenate(
        [jnp.broadcast_to(bl, (GLA_CHUNK, bl.shape[1])) for bl in b_last], axis=0)
    q_in = q * jnp.exp(b)
    k_in = k * jnp.exp(-b)
    k_st = k * jnp.exp(b_last_full - b)

    ri = lax.broadcasted_iota(jnp.int32, (ROWS, ROWS), 0)
    ci = lax.broadcasted_iota(jnp.int32, (ROWS, ROWS), 1)
    causal = (ci <= ri) & (ci >= ri - (ri & (GLA_CHUNK - 1)))
    row_id = lax.broadcasted_iota(jnp.int32, (SUBLANES, GLA_DK), 0)
    heads = range(GLA_HEADS)
    kcs = [slice(hd * GLA_DK, (hd + 1) * GLA_DK) for hd in heads]
    vcs = [slice(hd * GLA_DV, (hd + 1) * GLA_DV) for hd in heads]
    scores = [jnp.where(causal, _dot_nt(q_in[:, kcs[hd]], k_in[:, kcs[hd]]), 0.0) for hd in heads]
    kvs = [[_dot_tn(k_st[rows, kcs[hd]], v[rows, vcs[hd]]) for rows in chunk_rows] for hd in heads]
    next(mlp, None)
    crosses = []
    for hd in heads:
        tile = jnp.zeros((SUBLANES, GLA_DK), F32)
        for c in range(n_chunks):
            tile = jnp.where(row_id == c, b_last[c][:, kcs[hd]], tile)
        tile = jnp.concatenate([tile, jnp.zeros((GLA_DK - SUBLANES, GLA_DK), F32)], axis=0).T
        state = state_ref[hd]
        cross = []
        for c in range(n_chunks):
            cross.append(_dot(q_in[chunk_rows[c], kcs[hd]], state))
            state = jnp.exp(tile[:, c:c + 1]) * state + kvs[hd][c]
        state_ref[hd] = state
        crosses.append(jnp.concatenate(cross, axis=0))
    next(mlp, None)
    outs = [_dot(scores[hd], v[:, vcs[hd]]) + crosses[hd] for hd in heads]
    next(mlp, None)
    for hd in heads:
        o = outs[hd]
        y = o * lax.rsqrt(jnp.mean(o * o, axis=-1, keepdims=True) + NORM_EPS)
        mix_ref[:, vcs[hd]] = (y * ng_ref[:, vcs[hd]] * _silu(r[:, vcs[hd]])).astype(BF16)


def _odd_kernel(steps_per_seq, n_side, *refs):
    inputs, side_src, o_ref, side_dst, scratch = _split_refs(refs, ODD_INPUTS, n_side)
    (h_ref, gpre_ref, win_ref, wlow_ref, wgu_ref, gbias_ref, tri_ref, ng_ref, wout_ref, gpost_ref,
     gmlp_ref, wup_ref, wdown_ref, gmlp_post_ref) = inputs
    state_ref, mix_ref, h1_ref = scratch
    _side_cast(side_src, side_dst)
    step = pl.program_id(0)

    @pl.when(step % steps_per_seq == 0)
    def _():
        state_ref[...] = jnp.zeros_like(state_ref)

    run = functools.partial(_odd_step, step, inputs, o_ref, scratch)
    pl.when(step == 0)(lambda: run(False))
    pl.when(step > 0)(lambda: run(True))


def _odd_step(step, inputs, o_ref, scratch, run_mlp):
    (h_ref, gpre_ref, win_ref, wlow_ref, wgu_ref, gbias_ref, tri_ref, ng_ref, wout_ref, gpost_ref,
     gmlp_ref, wup_ref, wdown_ref, gmlp_post_ref) = inputs
    state_ref, mix_ref, h1_ref = scratch
    mlp = iter(())
    if run_mlp:
        mlp = _mlp_steps(h1_ref[(step + 1) % 2], gmlp_ref, wup_ref, wdown_ref, gmlp_post_ref, o_ref)
    x = h_ref[...]
    u = _rms(x, gpre_ref[...]).astype(BF16)
    next(mlp, None)

    def proj(a, b):
        return jnp.dot(u, win_ref[:, a:b], preferred_element_type=F32)

    a_low = jnp.dot(u, wlow_ref[...], preferred_element_type=F32)
    k = proj(OD_K, OD_V)
    gl = _dot(a_low, wgu_ref[...]) + gbias_ref[...]
    q = proj(OD_Q, OD_K) * (GLA_DK ** -0.5)
    log_g = (jnp.minimum(gl, 0.0) - jnp.log1p(jnp.exp(-jnp.abs(gl)))) * (1.0 / GLA_TAU)
    lg_hi = log_g.astype(BF16)
    lg_lo = (log_g - lg_hi.astype(F32)).astype(BF16)
    v = proj(OD_V, OD_R).astype(BF16)
    r = proj(OD_R, OD_END)
    for rows in _sub_block_rows():
        _gla(mlp, q[rows], k[rows], v[rows], r[rows], lg_hi[rows], lg_lo[rows], tri_ref, ng_ref,
             state_ref, mix_ref.at[rows])
    for _ in mlp:
        pass
    h1_ref[step % 2] = _out_proj(x, mix_ref, wout_ref, gpost_ref, ODD_OUT_PIECES)


def _odd_layer(h, g_pre, w_in, w_low, w_gu, g_bias, tri, ng, w_out, g_post,
               g_mlp, w_up, w_down, g_mlp_post, seq, layer, side_sources):
    nt = seq // STEP_ROWS
    n_blocks = h.shape[0] // STEP_ROWS
    in_spec, out_spec = _row_specs(n_blocks)
    side_in, side_out, side_shapes = _side_cast_specs(n_blocks, side_sources)
    return pl.pallas_call(
        functools.partial(_odd_kernel, nt, len(side_sources)),
        out_shape=[jax.ShapeDtypeStruct(h.shape, h.dtype)] + side_shapes,
        grid=(n_blocks + 1,),
        in_specs=[
            in_spec,
            _resident((1, D_MODEL)),
            _weight_spec(w_in, layer // 2),
            _stacked((D_MODEL, RANK_PAD), layer // 2),
            _stacked((RANK_PAD, GLA_HEADS * GLA_DK), layer // 2),
            _resident((1, GLA_HEADS * GLA_DK)),
            _resident((ROWS, ROWS)),
            _resident((1, D_MODEL)),
            _weight_spec(w_out, layer // 2),
            _resident((1, D_MODEL)),
            _resident((1, D_MODEL)),
            _weight_spec(w_up, layer),
            _weight_spec(w_down, layer),
            _resident((1, D_MODEL)),
        ] + side_in,
        out_specs=[out_spec] + side_out,
        scratch_shapes=[pltpu.VMEM((GLA_HEADS, GLA_DK, GLA_DV), F32)] + _layer_scratch(),
        compiler_params=pltpu.CompilerParams(
            dimension_semantics=("arbitrary",), vmem_limit_bytes=VMEM_LIMIT),
        name="odd_layer",
    )(h, g_pre, w_in, w_low, w_gu, g_bias, tri, ng, w_out, g_post, g_mlp, w_up, w_down, g_mlp_post,
      *[w for w, _, _ in side_sources])


def _retention_tables(seq):
    heads = jnp.arange(RET_HEADS, dtype=F32)
    log_gamma = jnp.log1p(-(2.0 ** (-5.0 - heads)))
    idx = jnp.arange(ROWS, dtype=F32)
    rel = idx[:, None] - idx[None, :]
    causal = rel >= 0
    dec = jnp.where(causal[None], jnp.exp(log_gamma[:, None, None] * jnp.where(causal, rel, 0.0)[None]), 0.0)
    k_dec = jnp.exp(log_gamma[None, :] * (ROWS - 1 - idx)[:, None])
    q_dec = jnp.exp(log_gamma[None, :] * (idx + 1.0)[:, None])
    c_dec = jnp.exp(log_gamma * ROWS)
    shape = (RET_HEADS, ROWS, RET_D)
    kdec = jnp.broadcast_to(k_dec.T[:, :, None], shape)
    qdec = jnp.broadcast_to(q_dec.T[:, :, None], shape)
    cdec = jnp.broadcast_to(c_dec[:, None, None], (RET_HEADS, RET_D, RET_D))
    half = RET_D // 2
    inv = ROPE_BASE ** (-np.arange(half, dtype=np.float64) / half)
    ang = np.arange(seq, dtype=np.float64)[:, None] * inv[None, :]
    cos, sin = np.cos(ang), np.sin(ang)
    cos2 = jnp.asarray(np.concatenate([cos, cos], axis=-1), F32)
    sin2 = jnp.asarray(np.concatenate([-sin, sin], axis=-1), F32)
    return dec, qdec, kdec, cdec, cos2, sin2


def kernel(x, norm_g, w_up, w_down, ev_w_in, ev_ret_norm_g, ev_sinks, ev_w_out, od_w_in,
           od_w_gate_up, od_gate_bias, od_norm_g, od_w_out):
    batch, seq, d = x.shape
    depth = norm_g.shape[0]
    h = x.reshape(batch * seq, d)
    dec, qdec, kdec, cdec, cos2, sin2 = _retention_tables(seq)
    pos = np.arange(ROWS)
    tri = jnp.asarray((pos[None, :] <= pos[:, None])
                      & (pos[None, :] // GLA_CHUNK == pos[:, None] // GLA_CHUNK), BF16)
    w_up_b, w_down_b = w_up[0].astype(BF16), w_down[0].astype(BF16)
    od_w_out_b = None
    od_w_in_b = od_w_in.astype(BF16)
    n_ev = ev_w_in.shape[0]
    aq_cols = ev_w_in[:, :, EV_AQ:EV_AK].reshape(n_ev, d, SWA_KV_HEADS, SWA_GROUP, SWA_HD)
    aq_cols = aq_cols.transpose(0, 1, 3, 2, 4).reshape(n_ev, d, EV_AK - EV_AQ)
    ev_w_in_b, ev_w_aq_b = ev_w_in.astype(BF16), aq_cols.astype(BF16)
    att_rows = ev_w_out[:, EV_RET_MIX:, :].reshape(n_ev, SWA_KV_HEADS, SWA_GROUP, SWA_HD, d)
    att_rows = att_rows.transpose(0, 2, 1, 3, 4).reshape(n_ev, SWA_HEADS * SWA_HD, d)
    ev_w_out_b = jnp.concatenate([ev_w_out[:, :EV_RET_MIX, :], att_rows], axis=1).astype(BF16)
    od_w_low_b = jnp.pad(od_w_in[:, :, OD_END:], ((0, 0), (0, 0), (0, RANK_PAD - GLA_RANK))).astype(BF16)
    od_w_gu_b = jnp.pad(od_w_gate_up, ((0, 0), (0, RANK_PAD - GLA_RANK), (0, 0))).astype(BF16)
    g = lambda layer, j: norm_g[layer, j].reshape(1, d)
    for layer in range(depth):
        i = layer // 2
        nxt = layer + 1
        side = []
        if nxt < depth:
            side = [(w_up, nxt, D_FF), (w_down, nxt, d)]
            if nxt % 2 == 1:
                side += [(od_w_out, nxt // 2, d)]
        mlp_args = (g(layer, 2), w_up_b, w_down_b, g(layer, 3), seq, layer, side)
        if layer % 2 == 0:
            h, *cast = _even_layer(h, ev_sinks[i], g(layer, 0), ev_w_in_b, ev_w_aq_b, cos2, sin2, dec, qdec,
                                   kdec, cdec, ev_ret_norm_g[i].reshape(1, -1), ev_w_out_b, g(layer, 1),
                                   *mlp_args)
        else:
            h, *cast = _odd_layer(h, g(layer, 0), od_w_in_b, od_w_low_b, od_w_gu_b,
                                  od_gate_bias[i].reshape(1, -1), tri, od_norm_g[i].reshape(1, -1),
                                  od_w_out_b, g(layer, 1), *mlp_args)
        if cast:
            w_up_b, w_down_b = cast[:2]
        if len(cast) == 3:
            od_w_out_b = cast[2]
    return h.reshape(batch, seq, d)
```
